```python
import math
import jax, jax.numpy as jnp
from jax import lax
import numpy as np

D_MODEL = 1024
BATCH = 8
SEQ = 4096
DEPTH = 4

CHUNK = 64
Q_BLOCK = 128
SPARSE_Q_BLOCK = 64
TOKEN_BLOCK = 128

D_MIX = D_MODEL
DIFF_WIDTH = D_MIX // 2
DSA_WIDTH = D_MIX - DIFF_WIDTH
DIFF_HEAD_DIM = 64
DIFF_V_DIM = 2 * DIFF_HEAD_DIM
DIFF_HEADS = DIFF_WIDTH // DIFF_V_DIM
DSA_HEAD_DIM = 64
DSA_HEADS = DSA_WIDTH // DSA_HEAD_DIM
IDX_HEADS = 4
IDX_DIM = 64
IDX_TOPK_MAX = 256

DIFF_QK_COLS = DIFF_HEADS * 2 * DIFF_HEAD_DIM
DIFF_V_COLS = DIFF_HEADS * DIFF_V_DIM
DSA_COLS = DSA_HEADS * DSA_HEAD_DIM
IDX_Q_COLS = IDX_HEADS * IDX_DIM
IN_WIDTHS = (DIFF_QK_COLS, DIFF_QK_COLS, DIFF_V_COLS, DSA_COLS, DSA_COLS, DSA_COLS, IDX_Q_COLS, IDX_DIM, IDX_HEADS)
V_SEGMENTS = (2, 5)
IN_COLS = sum(IN_WIDTHS)

PEER_HEADS = 8
PEER_N_KEYS = 128
PEER_N_EXPERTS = PEER_N_KEYS * PEER_N_KEYS
PEER_QUERY_DIM = 256
PEER_HALF_DIM = PEER_QUERY_DIM // 2
PEER_TOPK = 16

PLE_DIM = 256

LN_EPS = 1e-5
DEEPNORM_ALPHA = (2 * DEPTH) ** 0.25
DEEPNORM_BETA = (8 * DEPTH) ** -0.25
NEG_INF = -1e30

kernel_name = "hybrid_diffattn_dsa_peer_deepnorm"


def layer_norm(x, g, b):
    xf = x.astype(jnp.float32)
    mu = jnp.mean(xf, axis=-1, keepdims=True)
    xc = xf - mu
    var = jnp.mean(xc * xc, axis=-1, keepdims=True)
    y = xc * lax.rsqrt(var + LN_EPS) * g.astype(jnp.float32) + b.astype(jnp.float32)
    return y.astype(x.dtype)


def rms_norm(x, g):
    xf = x.astype(jnp.float32)
    y = xf * lax.rsqrt(jnp.mean(xf * xf, axis=-1, keepdims=True) + LN_EPS)
    return y * g.astype(jnp.float32)


def alibi_slopes(n):
    return 2.0 ** (-8.0 * jnp.arange(1, n + 1, dtype=jnp.float32) / n)


def chunk_end(t):
    return (t // CHUNK + 1) * CHUNK


def diff_attention(q, k, v, lam, slopes):
    B, S, H = q.shape[:3]
    nb = S // Q_BLOCK
    scale = DIFF_HEAD_DIM ** -0.5
    kf = k.astype(jnp.float32)
    vf = v.astype(jnp.float32)
    key_pos = jnp.arange(S)
    q_blocks = jnp.moveaxis(q.reshape(B, nb, Q_BLOCK, H, 2, DIFF_HEAD_DIM), 1, 0)

    def block(args):
        qb, bi = args
        t = bi * Q_BLOCK + jnp.arange(Q_BLOCK)
        visible = key_pos[None, :] < chunk_end(t)[:, None]
        bias = -slopes[:, None, None] * jnp.abs(t[:, None] - key_pos[None, :]).astype(jnp.float32)
        s = jnp.einsum('bqhmd,bkhmd->bmhqk', qb.astype(jnp.float32), kf) * scale + bias
        s = jnp.where(visible, s, NEG_INF)
        a = jax.nn.softmax(s, axis=-1)
        w = a[:, 0] - lam * a[:, 1]
        return jnp.einsum('bhqk,bkhe->bqhe', w, vf)

    out = lax.map(block, (q_blocks, jnp.arange(nb)))
    return jnp.moveaxis(out, 0, 1).reshape(B, S, H, DIFF_V_DIM)


def indexed_sparse_attention(q, k, v, qi, ki, wi, slopes, topk):
    B, S, H, d = q.shape
    nb = S // SPARSE_Q_BLOCK
    scale = DSA_HEAD_DIM ** -0.5
    kf = k.astype(jnp.float32)
    vf = v.astype(jnp.float32)
    kif = ki.astype(jnp.float32)
    key_pos = jnp.arange(S)

    def to_blocks(a):
        return jnp.moveaxis(a.reshape((B, nb, SPARSE_Q_BLOCK) + a.shape[2:]), 1, 0)

    def block(args):
        qb, qib, wib, bi = args
        t = bi * SPARSE_Q_BLOCK + jnp.arange(SPARSE_Q_BLOCK)
        end = chunk_end(t)
        visible = key_pos[None, :] < end[:, None]
        dots = jax.nn.relu(jnp.einsum('bqjd,bsd->bqjs', qib.astype(jnp.float32), kif))
        iscore = jnp.einsum('bqj,bqjs->bqs', wib.astype(jnp.float32), dots)
        iscore = jnp.where(visible[None], iscore, NEG_INF)
        _, sel = lax.top_k(iscore, topk)
        valid = sel < end[None, :, None]
        kg = jax.vmap(lambda kk, ii: kk[ii])(kf, sel)
        vg = jax.vmap(lambda vv, ii: vv[ii])(vf, sel)
        s = jnp.einsum('bqhd,bqkhd->bhqk', qb.astype(jnp.float32), kg) * scale
        dist = jnp.abs(t[None, :, None] - sel).astype(jnp.float32)
        s = s - slopes[None, :, None, None] * dist[:, None]
        s = jnp.where(valid[:, None], s, NEG_INF)
        a = jax.nn.softmax(s, axis=-1)
        return jnp.einsum('bhqk,bqkhd->bqhd', a, vg)

    out = lax.map(block, (to_blocks(q), to_blocks(qi), to_blocks(wi), jnp.arange(nb)))
    return jnp.moveaxis(out, 0, 1).reshape(B, S, H, d)


def token_mixers(h, w_in, w_out, lam_vecs, subln_g, lam_init):
    B, S, _ = h.shape
    proj = h @ w_in
    split_points = np.cumsum(IN_WIDTHS)[:-1].tolist()
    dq, dk, dv, sq, sk, sv, iq, ik, iw = jnp.split(proj, split_points, axis=-1)

    lv = lam_vecs.astype(jnp.float32)
    lam = jnp.exp(jnp.sum(lv[0] * lv[1])) - jnp.exp(jnp.sum(lv[2] * lv[3])) + lam_init
    dq = dq.reshape(B, S, DIFF_HEADS, 2, DIFF_HEAD_DIM)
    dk = dk.reshape(B, S, DIFF_HEADS, 2, DIFF_HEAD_DIM)
    dv = dv.reshape(B, S, DIFF_HEADS, DIFF_V_DIM)
    a_out = diff_attention(dq, dk, dv, lam, alibi_slopes(DIFF_HEADS))
    a_out = rms_norm(a_out, subln_g) * (1.0 - lam_init)

    topk = min(IDX_TOPK_MAX, S // 4)
    sq = sq.reshape(B, S, DSA_HEADS, DSA_HEAD_DIM)
    sk = sk.reshape(B, S, DSA_HEADS, DSA_HEAD_DIM)
    sv = sv.reshape(B, S, DSA_HEADS, DSA_HEAD_DIM)
    iq = iq.reshape(B, S, IDX_HEADS, IDX_DIM) * (IDX_DIM ** -0.5)
    iw = iw * (IDX_HEADS ** -0.5)
    b_out = indexed_sparse_attention(sq, sk, sv, iq, ik, iw, alibi_slopes(DSA_HEADS), topk)

    mixed = jnp.concatenate([a_out.reshape(B, S, DIFF_WIDTH), b_out.reshape(B, S, DSA_WIDTH)], axis=-1)
    return mixed.astype(h.dtype) @ w_out


def peer_ffn(h, w_q, sub_keys, u_tab, v_tab):
    B, S, D = h.shape
    nb = (B * S) // TOKEN_BLOCK
    xb = h.reshape(nb, TOKEN_BLOCK, D)
    skf = sub_keys.astype(jnp.float32)

    def block(xt):
        q = (xt @ w_q).reshape(TOKEN_BLOCK, PEER_HEADS, 2, PEER_HALF_DIM).astype(jnp.float32)
        s = jnp.einsum('thcd,hckd->thck', q, skf)
        s1, i1 = lax.top_k(s[:, :, 0], PEER_TOPK)
        s2, i2 = lax.top_k(s[:, :, 1], PEER_TOPK)
        cand = (s1[..., :, None] + s2[..., None, :]).reshape(TOKEN_BLOCK, PEER_HEADS, PEER_TOPK * PEER_TOPK)
        cidx = (i1[..., :, None] * PEER_N_KEYS + i2[..., None, :]).reshape(TOKEN_BLOCK, PEER_HEADS, PEER_TOPK * PEER_TOPK)
        best, pos = lax.top_k(cand, PEER_TOPK)
        eidx = jnp.take_along_axis(cidx, pos, axis=-1)
        g = jax.nn.softmax(best, axis=-1)
        u = u_tab[eidx]
        act = jax.nn.gelu(jnp.einsum('thkd,td->thk', u, xt).astype(jnp.float32), approximate=False)
        vv = v_tab[eidx]
        return jnp.einsum('thk,thkd->td', (g * act).astype(xt.dtype), vv)

    out = lax.map(block, xb)
    return out.reshape(B, S, D)


def setup_inputs(seed: int = 0) -> dict:
    key = jax.random.key(seed)
    ks = jax.random.split(key, 20)
    f32 = jnp.float32
    x = jax.random.normal(ks[0], (BATCH, SEQ, D_MODEL), f32)
    p = jax.random.normal(ks[1], (DEPTH, BATCH, SEQ, PLE_DIM), f32)
    ln_in_g = 1.0 + 0.02 * jax.random.normal(ks[2], (D_MODEL,), f32)
    ln_in_b = 0.02 * jax.random.normal(ks[3], (D_MODEL,), f32)
    col_scale = jnp.concatenate([
        jnp.full((w,), DEEPNORM_BETA if i in V_SEGMENTS else 1.0, f32) for i, w in enumerate(IN_WIDTHS)])
    w_in = jax.random.normal(ks[4], (DEPTH, D_MODEL, IN_COLS), f32) * (D_MODEL ** -0.5) * col_scale
    w_out = jax.random.normal(ks[5], (DEPTH, D_MIX, D_MODEL), f32) * (D_MIX ** -0.5) * DEEPNORM_BETA
    diff_lambda = 0.1 * jax.random.normal(ks[6], (DEPTH, 4, DIFF_HEAD_DIM), f32)
    diff_subln_g = 1.0 + 0.02 * jax.random.normal(ks[7], (DEPTH, DIFF_V_DIM), f32)
    ln1_g = 1.0 + 0.02 * jax.random.normal(ks[8], (DEPTH, D_MODEL), f32)
    ln1_b = 0.02 * jax.random.normal(ks[9], (DEPTH, D_MODEL), f32)
    peer_wq = jax.random.normal(ks[10], (DEPTH, D_MODEL, PEER_HEADS * PEER_QUERY_DIM), f32) * (D_MODEL ** -0.5)
    peer_subkeys = jax.random.normal(ks[11], (DEPTH, PEER_HEADS, 2, PEER_N_KEYS, PEER_HALF_DIM), f32) * (PEER_HALF_DIM ** -0.5)
    peer_u = jax.random.normal(ks[12], (DEPTH, PEER_N_EXPERTS, D_MODEL), f32) * (D_MODEL ** -0.5)
    peer_v = jax.random.normal(ks[13], (DEPTH, PEER_N_EXPERTS, D_MODEL), f32) * (PEER_HEADS ** -0.5) * DEEPNORM_BETA
    ple_wg = jax.random.normal(ks[14], (DEPTH, D_MODEL, D_MODEL), f32) * (D_MODEL ** -0.5)
    ple_bg = 0.02 * jax.random.normal(ks[15], (DEPTH, D_MODEL), f32)
    ple_wp = jax.random.normal(ks[16], (DEPTH, PLE_DIM, D_MODEL), f32) * (PLE_DIM ** -0.5) * DEEPNORM_BETA
    ln2_g = 1.0 + 0.02 * jax.random.normal(ks[17], (DEPTH, D_MODEL), f32)
    ln2_b = 0.02 * jax.random.normal(ks[18], (DEPTH, D_MODEL), f32)
    return {"x": x, "p": p, "ln_in_g": ln_in_g, "ln_in_b": ln_in_b, "w_in": w_in, "w_out": w_out,
            "diff_lambda": diff_lambda, "diff_subln_g": diff_subln_g, "ln1_g": ln1_g, "ln1_b": ln1_b,
            "peer_wq": peer_wq, "peer_subkeys": peer_subkeys, "peer_u": peer_u, "peer_v": peer_v,
            "ple_wg": ple_wg, "ple_bg": ple_bg, "ple_wp": ple_wp, "ln2_g": ln2_g, "ln2_b": ln2_b}


def reference(x, p, ln_in_g, ln_in_b, w_in, w_out, diff_lambda, diff_subln_g, ln1_g, ln1_b,
              peer_wq, peer_subkeys, peer_u, peer_v, ple_wg, ple_bg, ple_wp, ln2_g, ln2_b):
    h = layer_norm(x, ln_in_g, ln_in_b)
    for i in range(DEPTH):
        lam_init = 0.8 - 0.6 * math.exp(-0.3 * i)
        mix = token_mixers(h, w_in[i], w_out[i], diff_lambda[i], diff_subln_g[i], lam_init)
        h = layer_norm(DEEPNORM_ALPHA * h + mix, ln1_g[i], ln1_b[i])
        r = DEEPNORM_ALPHA * h + peer_ffn(h, peer_wq[i], peer_subkeys[i], peer_u[i], peer_v[i])
        gate = jax.nn.sigmoid((r @ ple_wg[i] + ple_bg[i]).astype(jnp.float32)).astype(r.dtype)
        r = r + gate * (p[i] @ ple_wp[i])
        h = layer_norm(r, ln2_g[i], ln2_b[i])
    return h
```

```python
import functools
import math
import struct

import jax
import jax.numpy as jnp
from jax import lax
from jax.experimental import pallas as pl
from jax.experimental.pallas import tpu as pltpu

F32 = jnp.float32
BF16 = jnp.bfloat16
I32 = jnp.int32

LN_EPS = 1e-5
NEG_INF = -1e30
CHUNK = 64
CHUNK_SHIFT = 6
LANES = 128
VMEM_LIMIT = 56 * 1024 * 1024

DIFF_HEAD_DIM = 64
DSA_HEAD_DIM = 64
IDX_HEADS = 4
IDX_DIM = 64
IDX_TOPK_MAX = 256
PEER_HEADS = 8
PEER_N_KEYS = 128
PEER_HALF_DIM = 128
PEER_TOPK = 16

_NEG_BITS = struct.unpack("<i", struct.pack("<f", NEG_INF))[0]
KEY_NEG = _NEG_BITS ^ 0x7FFFFFFF


def _cparams(n_axes):
    return pltpu.CompilerParams(
        dimension_semantics=("arbitrary",) * n_axes, vmem_limit_bytes=VMEM_LIMIT)


def _dot_nt(a, b):
    return lax.dot_general(a, b, (((1,), (1,)), ((), ())), preferred_element_type=F32)


def _layer_norm(x, g, b):
    mu = jnp.mean(x, axis=-1, keepdims=True)
    xc = x - mu
    var = jnp.mean(xc * xc, axis=-1, keepdims=True)
    return xc * lax.rsqrt(var + LN_EPS) * g + b


def _ln_kernel(x_ref, g_ref, b_ref, o_ref):
    o_ref[...] = _layer_norm(x_ref[...], g_ref[...], b_ref[...])


def _entry_layer_norm(x, g, b, tm=512):
    T, D = x.shape
    row = pl.BlockSpec((tm, D), lambda t: (t, 0))
    vec = pl.BlockSpec((1, D), lambda t: (0, 0))
    return pl.pallas_call(
        _ln_kernel, grid=(T // tm,), in_specs=[row, vec, vec], out_specs=row,
        out_shape=jax.ShapeDtypeStruct((T, D), F32), compiler_params=_cparams(1),
        name="entry_ln")(x, g.reshape(1, D), b.reshape(1, D))


def _inproj_kernel(h_ref, wm_ref, wiq_ref, wik_ref, wiw_ref, main_ref, iq_ref, ik_ref, iw_ref, *, tn):
    hb = h_ref[...].astype(BF16)
    for j in range(0, wm_ref.shape[1], tn):
        main_ref[:, j:j + tn] = jnp.dot(hb, wm_ref[:, j:j + tn], preferred_element_type=F32).astype(BF16)
    iq_ref[...] = jnp.dot(hb, wiq_ref[...], preferred_element_type=F32) * (IDX_DIM ** -0.5)
    ik_ref[...] = jnp.dot(hb, wik_ref[...], preferred_element_type=F32)
    iw_ref[...] = jnp.dot(hb, wiw_ref[...], preferred_element_type=F32) * (IDX_HEADS ** -0.5)


def _input_projection(h, w_main, w_iq, w_ik, w_iw, tm=512, tn=512):
    T, D = h.shape
    nm = w_main.shape[1]

    def full(w):
        return pl.BlockSpec(w.shape, lambda t: (0, 0))

    def rows(n):
        return pl.BlockSpec((tm, n), lambda t: (t, 0))

    return pl.pallas_call(
        functools.partial(_inproj_kernel, tn=tn), grid=(T // tm,),
        in_specs=[rows(D), full(w_main), full(w_iq), full(w_ik), full(w_iw)],
        out_specs=[rows(nm), rows(w_iq.shape[1]), rows(w_ik.shape[1]), rows(w_iw.shape[1])],
        out_shape=[jax.ShapeDtypeStruct((T, nm), BF16),
                   jax.ShapeDtypeStruct((T, w_iq.shape[1]), F32),
                   jax.ShapeDtypeStruct((T, w_ik.shape[1]), F32),
                   jax.ShapeDtypeStruct((T, w_iw.shape[1]), F32)],
        compiler_params=_cparams(1), name="in_proj")(h, w_main, w_iq, w_ik, w_iw)


def _online_softmax_step(s, v, m, l, acc):
    m_new = jnp.maximum(m, jnp.max(s, axis=-1, keepdims=True))
    p = jnp.exp(s - m_new)
    a = jnp.exp(m - m_new)
    l_new = a * l + jnp.sum(p, axis=-1, keepdims=True)
    acc_new = a * acc + jnp.dot(p.astype(BF16), v, preferred_element_type=F32)
    return m_new, l_new, acc_new


def _diff_kernel(slope_ref, lamv_ref, g_ref, q_ref, k_ref, v_ref, o_ref, *, tq, lam_init):
    hd = pl.program_id(1)
    qi = pl.program_id(2)
    tk = tq
    slope = slope_ref[hd]
    lane = lax.broadcasted_iota(I32, (tq, 2 * DIFF_HEAD_DIM), 1)
    qs = q_ref[0] * (DIFF_HEAD_DIM ** -0.5)
    zero = jnp.zeros_like(qs)
    qmaps = (jnp.where(lane < DIFF_HEAD_DIM, qs, zero), jnp.where(lane >= DIFF_HEAD_DIM, qs, zero))
    t_idx = qi * tq + lax.broadcasted_iota(I32, (tq, tk), 0)

    def step(kb, carry, diagonal):
        off = pl.multiple_of(kb * tk, tk)
        k = k_ref[0, pl.ds(off, tk), :]
        v = v_ref[0, pl.ds(off, tk), :]
        s_idx = off + lax.broadcasted_iota(I32, (tq, tk), 1)
        bias = -slope * jnp.abs(t_idx - s_idx).astype(F32)
        out = []
        for mp in range(2):
            s = _dot_nt(qmaps[mp], k) + bias
            if diagonal:
                visible = s_idx < (((t_idx >> CHUNK_SHIFT) + 1) << CHUNK_SHIFT)
                s = jnp.where(visible, s, NEG_INF)
            out.extend(_online_softmax_step(s, v, *carry[3 * mp:3 * mp + 3]))
        return tuple(out)

    m0 = jnp.full((tq, 1), -jnp.inf, F32)
    l0 = jnp.zeros((tq, 1), F32)
    a0 = jnp.zeros((tq, 2 * DIFF_HEAD_DIM), F32)
    carry = lax.fori_loop(0, qi, functools.partial(step, diagonal=False), (m0, l0, a0, m0, l0, a0))
    _, l_a, acc_a, _, l_b, acc_b = step(qi, carry, True)

    lv = lamv_ref[...]
    lam = (jnp.exp(jnp.sum(lv[0:1] * lv[1:2], axis=-1, keepdims=True))
           - jnp.exp(jnp.sum(lv[2:3] * lv[3:4], axis=-1, keepdims=True)) + lam_init)
    o = acc_a / l_a - lam * (acc_b / l_b)
    o = o * lax.rsqrt(jnp.mean(o * o, axis=-1, keepdims=True) + LN_EPS) * g_ref[...]
    o_ref[0] = (o * (1.0 - lam_init)).astype(o_ref.dtype)


def _diff_attention(main, lam_vecs, subln_g, lam_init, n_heads, col0, tq=256):
    B, S, _ = main.shape
    dv = 2 * DIFF_HEAD_DIM
    slopes = jnp.asarray([2.0 ** (-8.0 * (i + 1) / n_heads) for i in range(n_heads)], F32)
    kv = lambda base: pl.BlockSpec((1, S, dv), lambda b, h, q: (b, 0, base + h))
    qo = lambda base: pl.BlockSpec((1, tq, dv), lambda b, h, q: (b, q, base + h))
    return pl.pallas_call(
        functools.partial(_diff_kernel, tq=tq, lam_init=lam_init),
        grid=(B, n_heads, S // tq),
        in_specs=[pl.BlockSpec(memory_space=pltpu.SMEM),
                  pl.BlockSpec(lam_vecs.shape, lambda b, h, q: (0, 0)),
                  pl.BlockSpec((1, dv), lambda b, h, q: (0, 0)),
                  qo(col0), kv(col0 + n_heads), kv(col0 + 2 * n_heads)],
        out_specs=qo(0),
        out_shape=jax.ShapeDtypeStruct((B, S, n_heads * dv), BF16),
        compiler_params=_cparams(3), name="diff_attn")(
            slopes, lam_vecs, subln_g.reshape(1, dv), main, main, main)


def _sort_key(x):
    bits = pltpu.bitcast(x + 0.0, I32)
    return jnp.where(bits < 0, bits ^ 0x7FFFFFFF, bits)


def _dsa_kernel(q_ref, k_ref, v_ref, iq_ref, ik_ref, iw_ref, o_ref, key_sc, mask_sc, *,
                tq, tk, seq, topk, slopes):
    qi = pl.program_id(1)
    q_pos0 = qi * tq
    nkb = (q_pos0 + tq + tk - 1) // tk
    n_tail = seq - nkb * tk
    t_idx = q_pos0 + lax.broadcasted_iota(I32, (tq, tk), 0)
    chunk_end = ((t_idx >> CHUNK_SHIFT) + 1) << CHUNK_SHIFT

    iq = iq_ref[0].astype(BF16)
    iw = iw_ref[0]
    iq_heads = [iq[:, j * IDX_DIM:(j + 1) * IDX_DIM] for j in range(IDX_HEADS)]
    iw_heads = [iw[:, j:j + 1] for j in range(IDX_HEADS)]

    def score_block(kb, _):
        off = pl.multiple_of(kb * tk, tk)
        ik = ik_ref[0, pl.ds(off, tk), :].astype(BF16)
        acc = jnp.zeros((tq, tk), F32)
        for j in range(IDX_HEADS):
            acc = acc + iw_heads[j] * jnp.maximum(_dot_nt(iq_heads[j], ik), 0.0)
        s_idx = off + lax.broadcasted_iota(I32, (tq, tk), 1)
        key_sc[kb] = _sort_key(jnp.where(s_idx < chunk_end, acc, NEG_INF))
        return 0

    lax.fori_loop(0, nkb, score_block, 0)

    def count(pred):
        def body(kb, c):
            return c + jnp.sum(pred(key_sc[kb]).astype(I32), axis=-1, keepdims=True)
        return lax.fori_loop(0, nkb, body, jnp.zeros((tq, 1), I32))

    def bit_step(i, r):
        cand = r ^ (jnp.int32(1) << (31 - i))
        cnt = count(lambda kk: kk >= cand) + jnp.where(cand <= KEY_NEG, n_tail, 0)
        return jnp.where(cnt >= topk, cand, r)

    tau = lax.fori_loop(0, 32, bit_step, jnp.full((tq, 1), -2 ** 31, I32))
    cnt_gt = count(lambda kk: kk > tau) + jnp.where(tau < KEY_NEG, n_tail, 0)
    cnt_eq = count(lambda kk: kk == tau) + jnp.where(tau == KEY_NEG, n_tail, 0)
    need = topk - cnt_gt
    has_ties = jnp.max(cnt_eq - need) > 0

    def visible_of(kb):
        s_idx = kb * tk + lax.broadcasted_iota(I32, (tq, tk), 1)
        return s_idx < chunk_end

    @pl.when(jnp.logical_not(has_ties))
    def _():
        def body(kb, _):
            sel = jnp.logical_and(key_sc[kb] >= tau, visible_of(kb))
            mask_sc[kb] = jnp.where(sel, 0.0, NEG_INF)
            return 0
        lax.fori_loop(0, nkb, body, 0)

    @pl.when(has_ties)
    def _():
        r_i = lax.broadcasted_iota(I32, (tk, tk), 0)
        c_i = lax.broadcasted_iota(I32, (tk, tk), 1)
        before = jnp.where(r_i < c_i, 1.0, 0.0).astype(BF16)

        def body(kb, seen):
            kk = key_sc[kb]
            eq = kk == tau
            eqf = jnp.where(eq, 1.0, 0.0)
            rank = seen + jnp.dot(eqf.astype(BF16), before, preferred_element_type=F32)
            take = jnp.logical_and(eq, rank < need.astype(F32))
            sel = jnp.logical_and(jnp.logical_or(kk > tau, take), visible_of(kb))
            mask_sc[kb] = jnp.where(sel, 0.0, NEG_INF)
            return seen + jnp.sum(eqf, axis=-1, keepdims=True)
        lax.fori_loop(0, nkb, body, jnp.zeros((tq, 1), F32))

    pair = 2 * DSA_HEAD_DIM
    lane = lax.broadcasted_iota(I32, (tq, pair), 1)
    for p in range(len(slopes) // 2):
        qs = q_ref[0, :, p * pair:(p + 1) * pair] * (DSA_HEAD_DIM ** -0.5)
        zero = jnp.zeros_like(qs)
        q_lo = jnp.where(lane < DSA_HEAD_DIM, qs, zero)
        q_hi = jnp.where(lane >= DSA_HEAD_DIM, qs, zero)

        def attend(kb, carry, p=p, q_lo=q_lo, q_hi=q_hi):
            off = pl.multiple_of(kb * tk, tk)
            k = k_ref[0, pl.ds(off, tk), p * pair:(p + 1) * pair]
            v = v_ref[0, pl.ds(off, tk), p * pair:(p + 1) * pair]
            s_idx = off + lax.broadcasted_iota(I32, (tq, tk), 1)
            dist = jnp.abs(t_idx - s_idx).astype(F32)
            madd = mask_sc[kb]
            lo = _online_softmax_step(_dot_nt(q_lo, k) - slopes[2 * p] * dist + madd, v, *carry[:3])
            hi = _online_softmax_step(_dot_nt(q_hi, k) - slopes[2 * p + 1] * dist + madd, v, *carry[3:])
            return lo + hi

        m0 = jnp.full((tq, 1), -jnp.inf, F32)
        l0 = jnp.zeros((tq, 1), F32)
        a0 = jnp.zeros((tq, pair), F32)
        _, l_lo, acc_lo, _, l_hi, acc_hi = lax.fori_loop(0, nkb, attend, (m0, l0, a0, m0, l0, a0))
        o = jnp.where(lane < DSA_HEAD_DIM, acc_lo / l_lo, acc_hi / l_hi)
        o_ref[0, :, p * pair:(p + 1) * pair] = o.astype(o_ref.dtype)


def _dsa_attention(main, iq, ik, iw, n_heads, col0, tq=128, tk=512):
    B, S, _ = main.shape
    tk = min(tk, S)
    width = n_heads * DSA_HEAD_DIM
    topk = min(IDX_TOPK_MAX, S // 4)
    slopes = tuple(2.0 ** (-8.0 * (i + 1) / n_heads) for i in range(n_heads))
    qrow = lambda n, c: pl.BlockSpec((1, tq, n), lambda b, q: (b, q, c))
    seqb = lambda n, c: pl.BlockSpec((1, S, n), lambda b, q: (b, 0, c))
    return pl.pallas_call(
        functools.partial(_dsa_kernel, tq=tq, tk=tk, seq=S, topk=topk, slopes=slopes),
        grid=(B, S // tq),
        in_specs=[qrow(width, col0), seqb(width, col0 + 1), seqb(width, col0 + 2),
                  qrow(iq.shape[-1], 0), seqb(ik.shape[-1], 0), qrow(iw.shape[-1], 0)],
        out_specs=qrow(width, 0),
        out_shape=jax.ShapeDtypeStruct((B, S, width), BF16),
        scratch_shapes=[pltpu.VMEM((S // tk, tq, tk), I32), pltpu.VMEM((S // tk, tq, tk), F32)],
        compiler_params=_cparams(2), name="dsa_attn")(main, main, main, iq, ik, iw)


def _outproj_kernel(a_ref, b_ref, h_ref, w_ref, g_ref, beta_ref, o_ref, *, alpha):
    wa = a_ref.shape[1]
    y = jnp.dot(a_ref[...], w_ref[:wa, :], preferred_element_type=F32)
    y = y + jnp.dot(b_ref[...], w_ref[wa:, :], preferred_element_type=F32)
    o_ref[...] = _layer_norm(alpha * h_ref[...] + y, g_ref[...], beta_ref[...])


def _output_projection(a, b, h, w_out, g, beta, alpha, tm=512):
    T, D = h.shape
    rows = lambda n: pl.BlockSpec((tm, n), lambda t: (t, 0))
    vec = pl.BlockSpec((1, D), lambda t: (0, 0))
    return pl.pallas_call(
        functools.partial(_outproj_kernel, alpha=alpha), grid=(T // tm,),
        in_specs=[rows(a.shape[1]), rows(b.shape[1]), rows(D),
                  pl.BlockSpec(w_out.shape, lambda t: (0, 0)), vec, vec],
        out_specs=rows(D), out_shape=jax.ShapeDtypeStruct((T, D), F32),
        compiler_params=_cparams(1), name="out_proj_ln1")(
            a, b, h, w_out, g.reshape(1, D), beta.reshape(1, D))


def _top_rows(x, n):
    rows = []
    for _ in range(n):
        mx = jnp.max(x, axis=0, keepdims=True)
        rows.append(mx)
        x = jnp.where(x == mx, -jnp.inf, x)
    return rows


def _route_kernel(h_ref, wq_ref, sk_ref, thr_ref, c1_ref, s2_ref, e2_ref):
    hb = h_ref[...].astype(BF16)
    qd = 2 * PEER_HALF_DIM
    for hd in range(PEER_HEADS):
        q = jnp.dot(hb, wq_ref[:, hd * qd:(hd + 1) * qd], preferred_element_type=F32).astype(BF16)
        s1 = _dot_nt(sk_ref[hd, 0], q[:, :PEER_HALF_DIM])
        s2 = _dot_nt(sk_ref[hd, 1], q[:, PEER_HALF_DIM:])
        top1 = _top_rows(s1, PEER_TOPK)
        top2 = jnp.concatenate(_top_rows(s2, PEER_TOPK), axis=0)
        cand = jnp.concatenate([r + top2 for r in top1], axis=0)
        best = _top_rows(cand, PEER_TOPK + 1)
        z = jnp.ones_like(best[0])
        for r in best[1:PEER_TOPK]:
            z = z + jnp.exp(r - best[0])
        cut = 0.5 * (best[PEER_TOPK - 1] + best[PEER_TOPK])
        thr_ref[hd] = cut - s1
        c1_ref[hd] = jnp.where(s1 >= top1[-1], jnp.exp(s1 - top1[0]) / z, 0.0)
        s2_ref[hd] = s2
        e2_ref[hd] = jnp.where(s2 >= top2[PEER_TOPK - 1:], jnp.exp(s2 - top2[0:1]), 0.0)


def _peer_route(h, wq, subkeys, tm=256):
    T, D = h.shape
    out = jax.ShapeDtypeStruct((PEER_HEADS, PEER_N_KEYS, T), F32)
    ospec = pl.BlockSpec((PEER_HEADS, PEER_N_KEYS, tm), lambda t: (0, 0, t))
    return pl.pallas_call(
        _route_kernel, grid=(T // tm,),
        in_specs=[pl.BlockSpec((tm, D), lambda t: (t, 0)),
                  pl.BlockSpec(wq.shape, lambda t: (0, 0)),
                  pl.BlockSpec(subkeys.shape, lambda t: (0, 0, 0, 0))],
        out_specs=[ospec] * 4, out_shape=[out] * 4,
        compiler_params=_cparams(1), name="peer_route")(h, wq, subkeys)


def _gelu(x):
    return 0.5 * x * (1.0 + lax.erf(x * (2.0 ** -0.5)))


def _peer_kernel(h_ref, u_ref, vt_ref, thr_ref, c1_ref, s2_ref, e2_ref, o_ref, xb_sc, acc_sc, w_sc, *, te):
    j = pl.program_id(1)

    @pl.when(j == 0)
    def _():
        xb_sc[...] = h_ref[...].astype(BF16)
        acc_sc[...] = jnp.zeros_like(acc_sc)

    act = _gelu(_dot_nt(u_ref[...], xb_sc[...]))
    per = te // PEER_N_KEYS
    for ii in range(per):
        i = j * per + ii
        gate = jnp.zeros((PEER_N_KEYS, act.shape[1]), F32)
        for hd in range(PEER_HEADS):
            thr = thr_ref[hd, pl.ds(i, 1), :]
            c1 = c1_ref[hd, pl.ds(i, 1), :]
            gate = gate + jnp.where(s2_ref[hd] >= thr, e2_ref[hd] * c1, 0.0)
        rows = slice(ii * PEER_N_KEYS, (ii + 1) * PEER_N_KEYS)
        w_sc[rows, :] = (gate * act[rows, :]).astype(BF16)
    acc_sc[...] += jnp.dot(vt_ref[...], w_sc[...], preferred_element_type=F32)

    @pl.when(j == pl.num_programs(1) - 1)
    def _():
        o_ref[...] = acc_sc[...].T


def _peer_experts(h, u_bf, vt_bf, thr, c1, s2, e2, tm=512, te=512):
    T, D = h.shape
    E = u_bf.shape[0]
    rspec = pl.BlockSpec((PEER_HEADS, PEER_N_KEYS, tm), lambda t, j: (0, 0, t))
    return pl.pallas_call(
        functools.partial(_peer_kernel, te=te), grid=(T // tm, E // te),
        in_specs=[pl.BlockSpec((tm, D), lambda t, j: (t, 0)),
                  pl.BlockSpec((te, D), lambda t, j: (j, 0)),
                  pl.BlockSpec((D, te), lambda t, j: (0, j)),
                  rspec, rspec, rspec, rspec],
        out_specs=pl.BlockSpec((tm, D), lambda t, j: (t, 0)),
        out_shape=jax.ShapeDtypeStruct((T, D), F32),
        scratch_shapes=[pltpu.VMEM((tm, D), BF16), pltpu.VMEM((D, tm), F32), pltpu.VMEM((te, tm), BF16)],
        compiler_params=_cparams(2), name="peer_experts")(h, u_bf, vt_bf, thr, c1, s2, e2)


def _ple_kernel(h_ref, f_ref, p_ref, wg_ref, bg_ref, wp_ref, g_ref, beta_ref, o_ref, *, alpha):
    r = alpha * h_ref[...] + f_ref[...]
    z = jnp.dot(r.astype(BF16), wg_ref[...], preferred_element_type=F32) + bg_ref[...]
    gate = 1.0 / (1.0 + jnp.exp(-z))
    r = r + gate * jnp.dot(p_ref[...].astype(BF16), wp_ref[...], preferred_element_type=F32)
    o_ref[...] = _layer_norm(r, g_ref[...], beta_ref[...])


def _ple_ln2(h, f, p, layer, wg, bg, wp, g, beta, alpha, tm=512):
    T, D = h.shape
    pd = p.shape[-1]
    rows = pl.BlockSpec((tm, D), lambda t: (t, 0))
    vec = pl.BlockSpec((1, D), lambda t: (0, 0))
    return pl.pallas_call(
        functools.partial(_ple_kernel, alpha=alpha), grid=(T // tm,),
        in_specs=[rows, rows, pl.BlockSpec((None, tm, pd), lambda t: (layer, t, 0)),
                  pl.BlockSpec(wg.shape, lambda t: (0, 0)), vec,
                  pl.BlockSpec(wp.shape, lambda t: (0, 0)), vec, vec],
        out_specs=rows, out_shape=jax.ShapeDtypeStruct((T, D), F32),
        compiler_params=_cparams(1), name="ple_ln2")(
            h, f, p, wg, bg.reshape(1, D), wp, g.reshape(1, D), beta.reshape(1, D))


def _mixer_layer(h, B, S, w_in, w_out, lam_vecs, subln_g, lam_init, ln_g, ln_b, alpha):
    T, D = h.shape
    n_diff = (D // 2) // (2 * DIFF_HEAD_DIM)
    n_dsa = (D - D // 2) // DSA_HEAD_DIM
    n_main = 3 * (D // 2) + 3 * n_dsa * DSA_HEAD_DIM
    n_iq = IDX_HEADS * IDX_DIM
    wb = w_in.astype(BF16)
    w_main = wb[:, :n_main]
    w_iq = wb[:, n_main:n_main + n_iq]
    w_ik = wb[:, n_main + n_iq:n_main + n_iq + IDX_DIM]
    w_iw = jnp.pad(wb[:, n_main + n_iq + IDX_DIM:], ((0, 0), (0, LANES - IDX_HEADS)))
    main, iq, ik, iw = _input_projection(h, w_main, w_iq, w_ik, w_iw)
    main = main.reshape(B, S, n_main)
    a = _diff_attention(main, lam_vecs, subln_g, lam_init, n_diff, 0)
    b = _dsa_attention(main, iq.reshape(B, S, -1), ik.reshape(B, S, -1), iw.reshape(B, S, -1), n_dsa, 3)
    return _output_projection(a.reshape(T, -1), b.reshape(T, -1), h, w_out.astype(BF16), ln_g, ln_b, alpha)


def kernel(x, p, ln_in_g, ln_in_b, w_in, w_out, diff_lambda, diff_subln_g, ln1_g, ln1_b, peer_wq, peer_subkeys, peer_u, peer_v, ple_wg, ple_bg, ple_wp, ln2_g, ln2_b):
    B, S, D = x.shape
    depth = w_in.shape[0]
    T = B * S
    alpha = (2 * depth) ** 0.25
    h = _entry_layer_norm(x.reshape(T, D), ln_in_g, ln_in_b)
    p2 = p.reshape(depth, T, p.shape[-1])
    for i in range(depth):
        lam_init = 0.8 - 0.6 * math.exp(-0.3 * i)
        h = _mixer_layer(h, B, S, w_in[i], w_out[i], diff_lambda[i], diff_subln_g[i], lam_init,
                         ln1_g[i], ln1_b[i], alpha)
        thr, c1, s2, e2 = _peer_route(h, peer_wq[i].astype(BF16), peer_subkeys[i].astype(BF16))
        f = _peer_experts(h, peer_u[i].astype(BF16), peer_v[i].astype(BF16).T, thr, c1, s2, e2)
        h = _ple_ln2(h, f, p2, i, ple_wg[i].astype(BF16), ple_bg[i], ple_wp[i].astype(BF16),
                     ln2_g[i], ln2_b[i], alpha)
    return h.reshape(B, S, D)
```

```python
import functools
import math
import struct

import jax
import jax.numpy as jnp
from jax import lax
from jax.experimental import pallas as pl
from jax.experimental.pallas import tpu as pltpu

F32 = jnp.float32
BF16 = jnp.bfloat16
I32 = jnp.int32

LN_EPS = 1e-5
NEG_INF = -1e30
CHUNK = 64
CHUNK_SHIFT = 6
LANES = 128
VMEM_LIMIT = 56 * 1024 * 1024

DIFF_HEAD_DIM = 64
DSA_HEAD_DIM = 64
IDX_HEADS = 4
IDX_DIM = 64
IDX_TOPK_MAX = 256
PEER_HEADS = 8
PEER_N_KEYS = 128
PEER_HALF_DIM = 128
PEER_TOPK = 16

_NEG_BITS = struct.unpack("<i", struct.pack("<f", NEG_INF))[0]
KEY_NEG = _NEG_BITS ^ 0x7FFFFFFF


def _cparams(n_axes):
    return pltpu.CompilerParams(
        dimension_semantics=("arbitrary",) * n_axes, vmem_limit_bytes=VMEM_LIMIT)


def _dot_nt(a, b):
    return lax.dot_general(a, b, (((1,), (1,)), ((), ())), preferred_element_type=F32)


def _layer_norm(x, g, b):
    mu = jnp.mean(x, axis=-1, keepdims=True)
    xc = x - mu
    var = jnp.mean(xc * xc, axis=-1, keepdims=True)
    return xc * lax.rsqrt(var + LN_EPS) * g + b


def _ln_kernel(x_ref, g_ref, b_ref, o_ref):
    o_ref[...] = _layer_norm(x_ref[...], g_ref[...], b_ref[...])


def _entry_layer_norm(x, g, b, tm=512):
    T, D = x.shape
    row = pl.BlockSpec((tm, D), lambda t: (t, 0))
    vec = pl.BlockSpec((1, D), lambda t: (0, 0))
    return pl.pallas_call(
        _ln_kernel, grid=(T // tm,), in_specs=[row, vec, vec], out_specs=row,
        out_shape=jax.ShapeDtypeStruct((T, D), F32), compiler_params=_cparams(1),
        name="entry_ln")(x, g.reshape(1, D), b.reshape(1, D))


def _inproj_kernel(h_ref, wm_ref, wiq_ref, wik_ref, wiw_ref, main_ref, iq_ref, ik_ref, iw_ref, *, tn):
    hb = h_ref[...].astype(BF16)
    for j in range(0, wm_ref.shape[1], tn):
        main_ref[:, j:j + tn] = jnp.dot(hb, wm_ref[:, j:j + tn], preferred_element_type=F32).astype(BF16)
    iq_ref[...] = jnp.dot(hb, wiq_ref[...], preferred_element_type=F32) * (IDX_DIM ** -0.5)
    ik_ref[...] = jnp.dot(hb, wik_ref[...], preferred_element_type=F32)
    iw_ref[...] = jnp.dot(hb, wiw_ref[...], preferred_element_type=F32) * (IDX_HEADS ** -0.5)


def _input_projection(h, w_main, w_iq, w_ik, w_iw, tm=512, tn=512):
    T, D = h.shape
    nm = w_main.shape[1]

    def full(w):
        return pl.BlockSpec(w.shape, lambda t: (0, 0))

    def rows(n):
        return pl.BlockSpec((tm, n), lambda t: (t, 0))

    return pl.pallas_call(
        functools.partial(_inproj_kernel, tn=tn), grid=(T // tm,),
        in_specs=[rows(D), full(w_main), full(w_iq), full(w_ik), full(w_iw)],
        out_specs=[rows(nm), rows(w_iq.shape[1]), rows(w_ik.shape[1]), rows(w_iw.shape[1])],
        out_shape=[jax.ShapeDtypeStruct((T, nm), BF16),
                   jax.ShapeDtypeStruct((T, w_iq.shape[1]), F32),
                   jax.ShapeDtypeStruct((T, w_ik.shape[1]), F32),
                   jax.ShapeDtypeStruct((T, w_iw.shape[1]), F32)],
        compiler_params=_cparams(1), name="in_proj")(h, w_main, w_iq, w_ik, w_iw)


def _diff_kernel(lamv_ref, g_ref, q_ref, k_ref, v_ref, o_ref, qm_sc, acc_sc, *, tq, lam_init, slopes):
    qi = pl.program_id(1)
    tk = tq
    dv = 2 * DIFF_HEAD_DIM
    n_heads = len(slopes)
    lane = lax.broadcasted_iota(I32, (tq, dv), 1)
    for hd in range(n_heads):
        qs = q_ref[0, :, hd * dv:(hd + 1) * dv] * (DIFF_HEAD_DIM ** -0.5)
        zero = jnp.zeros_like(qs)
        qm_sc[2 * hd] = jnp.where(lane < DIFF_HEAD_DIM, qs, zero)
        qm_sc[2 * hd + 1] = jnp.where(lane >= DIFF_HEAD_DIM, qs, zero)
    acc_sc[...] = jnp.zeros_like(acc_sc)
    t_idx = qi * tq + lax.broadcasted_iota(I32, (tq, tk), 0)

    def step(kb, carry, diagonal):
        off = pl.multiple_of(kb * tk, tk)
        s_idx = off + lax.broadcasted_iota(I32, (tq, tk), 1)
        dist = jnp.abs(t_idx - s_idx).astype(F32)
        if diagonal:
            visible = s_idx < (((t_idx >> CHUNK_SHIFT) + 1) << CHUNK_SHIFT)
        out = []
        for hd in range(n_heads):
            k = k_ref[0, pl.ds(off, tk), hd * dv:(hd + 1) * dv]
            v = v_ref[0, pl.ds(off, tk), hd * dv:(hd + 1) * dv]
            bias = -slopes[hd] * dist
            for mp in range(2):
                c = 2 * hd + mp
                s = _dot_nt(qm_sc[c], k) + bias
                if diagonal:
                    s = jnp.where(visible, s, NEG_INF)
                m, l = carry[2 * c], carry[2 * c + 1]
                m_new = jnp.maximum(m, jnp.max(s, axis=-1, keepdims=True))
                p = jnp.exp(s - m_new)
                a = jnp.exp(m - m_new)
                acc_sc[c] = a * acc_sc[c] + jnp.dot(p.astype(BF16), v, preferred_element_type=F32)
                out.extend((m_new, a * l + jnp.sum(p, axis=-1, keepdims=True)))
        return tuple(out)

    m0 = jnp.full((tq, 1), -jnp.inf, F32)
    l0 = jnp.zeros((tq, 1), F32)
    carry = lax.fori_loop(0, qi, functools.partial(step, diagonal=False), (m0, l0) * (2 * n_heads))
    carry = step(qi, carry, True)

    lv = lamv_ref[...]
    lam = (jnp.exp(jnp.sum(lv[0:1] * lv[1:2], axis=-1, keepdims=True))
           - jnp.exp(jnp.sum(lv[2:3] * lv[3:4], axis=-1, keepdims=True)) + lam_init)
    for hd in range(n_heads):
        o = acc_sc[2 * hd] / carry[4 * hd + 1] - lam * (acc_sc[2 * hd + 1] / carry[4 * hd + 3])
        o = o * lax.rsqrt(jnp.mean(o * o, axis=-1, keepdims=True) + LN_EPS) * g_ref[...]
        o_ref[0, :, hd * dv:(hd + 1) * dv] = (o * (1.0 - lam_init)).astype(o_ref.dtype)


def _diff_attention(main, lam_vecs, subln_g, lam_init, n_heads, col0, tq=256):
    B, S, _ = main.shape
    dv = 2 * DIFF_HEAD_DIM
    width = n_heads * dv
    slopes = tuple(2.0 ** (-8.0 * (i + 1) / n_heads) for i in range(n_heads))
    kv = lambda c: pl.BlockSpec((1, S, width), lambda b, q: (b, 0, c))
    qo = lambda c: pl.BlockSpec((1, tq, width), lambda b, q: (b, q, c))
    return pl.pallas_call(
        functools.partial(_diff_kernel, tq=tq, lam_init=lam_init, slopes=slopes),
        grid=(B, S // tq),
        in_specs=[pl.BlockSpec(lam_vecs.shape, lambda b, q: (0, 0)),
                  pl.BlockSpec((1, dv), lambda b, q: (0, 0)),
                  qo(col0), kv(col0 + 1), kv(col0 + 2)],
        out_specs=qo(0),
        out_shape=jax.ShapeDtypeStruct((B, S, width), BF16),
        scratch_shapes=[pltpu.VMEM((2 * n_heads, tq, dv), BF16), pltpu.VMEM((2 * n_heads, tq, dv), F32)],
        compiler_params=_cparams(2), name="diff_attn")(
            lam_vecs, subln_g.reshape(1, dv), main, main, main)


def _sort_key(x):
    bits = pltpu.bitcast(x + 0.0, I32)
    return jnp.where(bits < 0, bits ^ 0x7FFFFFFF, bits)


def _dsa_kernel(q_ref, k_ref, v_ref, iq_ref, ik_ref, iw_ref, o_ref, key_sc, mask_sc, qm_sc, acc_sc, *,
                tq, tk, seq, topk, slopes):
    qi = pl.program_id(1)
    q_pos0 = qi * tq
    nkb = (q_pos0 + tq + tk - 1) // tk
    n_tail = seq - nkb * tk
    t_idx = q_pos0 + lax.broadcasted_iota(I32, (tq, tk), 0)
    chunk_end = ((t_idx >> CHUNK_SHIFT) + 1) << CHUNK_SHIFT

    iq = iq_ref[0].astype(BF16)
    iw = iw_ref[0]
    iq_heads = [iq[:, j * IDX_DIM:(j + 1) * IDX_DIM] for j in range(IDX_HEADS)]
    iw_heads = [iw[:, j:j + 1] for j in range(IDX_HEADS)]

    def score_block(kb, _):
        off = pl.multiple_of(kb * tk, tk)
        ik = ik_ref[0, pl.ds(off, tk), :].astype(BF16)
        acc = jnp.zeros((tq, tk), F32)
        for j in range(IDX_HEADS):
            acc = acc + iw_heads[j] * jnp.maximum(_dot_nt(iq_heads[j], ik), 0.0)
        s_idx = off + lax.broadcasted_iota(I32, (tq, tk), 1)
        key_sc[kb] = _sort_key(jnp.where(s_idx < chunk_end, acc, NEG_INF))
        return 0

    lax.fori_loop(0, nkb, score_block, 0)

    def count(pred):
        def body(kb, part):
            kk = key_sc[kb]
            for c in range(tk // LANES):
                part = part + pred(kk[:, c * LANES:(c + 1) * LANES]).astype(I32)
            return part
        part = lax.fori_loop(0, nkb, body, jnp.zeros((tq, LANES), I32))
        return jnp.sum(part, axis=-1, keepdims=True)

    def searching(st):
        i, _, done = st
        return jnp.logical_and(i < 32, jnp.min(done) == 0)

    def bit_step(st):
        i, r, done = st
        cand = jnp.where(done > 0, r, r ^ (jnp.int32(1) << (31 - i)))
        cnt = count(lambda kk: kk >= cand) + jnp.where(cand <= KEY_NEG, n_tail, 0)
        return i + 1, jnp.where(cnt >= topk, cand, r), jnp.where(cnt == topk, 1, done)

    _, tau, _ = lax.while_loop(
        searching, bit_step, (jnp.int32(0), jnp.full((tq, 1), -2 ** 31, I32), jnp.zeros((tq, 1), I32)))
    cnt_gt = count(lambda kk: kk > tau) + jnp.where(tau < KEY_NEG, n_tail, 0)
    cnt_eq = count(lambda kk: kk == tau) + jnp.where(tau == KEY_NEG, n_tail, 0)
    need = topk - cnt_gt
    has_ties = jnp.max(cnt_eq - need) > 0

    def visible_of(kb):
        s_idx = kb * tk + lax.broadcasted_iota(I32, (tq, tk), 1)
        return s_idx < chunk_end

    @pl.when(jnp.logical_not(has_ties))
    def _():
        def body(kb, _):
            sel = jnp.logical_and(key_sc[kb] >= tau, visible_of(kb))
            mask_sc[kb] = jnp.where(sel, 0.0, NEG_INF)
            return 0
        lax.fori_loop(0, nkb, body, 0)

    @pl.when(has_ties)
    def _():
        r_i = lax.broadcasted_iota(I32, (tk, tk), 0)
        c_i = lax.broadcasted_iota(I32, (tk, tk), 1)
        before = jnp.where(r_i < c_i, 1.0, 0.0).astype(BF16)

        def body(kb, seen):
            kk = key_sc[kb]
            eq = kk == tau
            eqf = jnp.where(eq, 1.0, 0.0)
            rank = seen + jnp.dot(eqf.astype(BF16), before, preferred_element_type=F32)
            take = jnp.logical_and(eq, rank < need.astype(F32))
            sel = jnp.logical_and(jnp.logical_or(kk > tau, take), visible_of(kb))
            mask_sc[kb] = jnp.where(sel, 0.0, NEG_INF)
            return seen + jnp.sum(eqf, axis=-1, keepdims=True)
        lax.fori_loop(0, nkb, body, jnp.zeros((tq, 1), F32))

    pair = 2 * DSA_HEAD_DIM
    n_heads = len(slopes)
    lane = lax.broadcasted_iota(I32, (tq, pair), 1)
    for p in range(n_heads // 2):
        qs = q_ref[0, :, p * pair:(p + 1) * pair] * (DSA_HEAD_DIM ** -0.5)
        zero = jnp.zeros_like(qs)
        qm_sc[2 * p] = jnp.where(lane < DSA_HEAD_DIM, qs, zero)
        qm_sc[2 * p + 1] = jnp.where(lane >= DSA_HEAD_DIM, qs, zero)
    acc_sc[...] = jnp.zeros_like(acc_sc)

    def attend(kb, carry):
        off = pl.multiple_of(kb * tk, tk)
        s_idx = off + lax.broadcasted_iota(I32, (tq, tk), 1)
        dist = jnp.abs(t_idx - s_idx).astype(F32)
        madd = mask_sc[kb]
        out = []
        for hd in range(n_heads):
            cols = slice((hd // 2) * pair, (hd // 2 + 1) * pair)
            k = k_ref[0, pl.ds(off, tk), cols]
            v = v_ref[0, pl.ds(off, tk), cols]
            s = _dot_nt(qm_sc[hd], k) - slopes[hd] * dist + madd
            m, l = carry[2 * hd], carry[2 * hd + 1]
            m_new = jnp.maximum(m, jnp.max(s, axis=-1, keepdims=True))
            pr = jnp.exp(s - m_new)
            a = jnp.exp(m - m_new)
            acc_sc[hd] = a * acc_sc[hd] + jnp.dot(pr.astype(BF16), v, preferred_element_type=F32)
            out.extend((m_new, a * l + jnp.sum(pr, axis=-1, keepdims=True)))
        return tuple(out)

    m0 = jnp.full((tq, 1), -jnp.inf, F32)
    l0 = jnp.zeros((tq, 1), F32)
    carry = lax.fori_loop(0, nkb, attend, (m0, l0) * n_heads)
    for p in range(n_heads // 2):
        o = jnp.where(lane < DSA_HEAD_DIM, acc_sc[2 * p] / carry[4 * p + 1], acc_sc[2 * p + 1] / carry[4 * p + 3])
        o_ref[0, :, p * pair:(p + 1) * pair] = o.astype(o_ref.dtype)


def _dsa_attention(main, iq, ik, iw, n_heads, col0, tq=128, tk=512):
    B, S, _ = main.shape
    tk = min(tk, S)
    width = n_heads * DSA_HEAD_DIM
    topk = min(IDX_TOPK_MAX, S // 4)
    slopes = tuple(2.0 ** (-8.0 * (i + 1) / n_heads) for i in range(n_heads))
    qrow = lambda n, c: pl.BlockSpec((1, tq, n), lambda b, q: (b, q, c))
    seqb = lambda n, c: pl.BlockSpec((1, S, n), lambda b, q: (b, 0, c))
    return pl.pallas_call(
        functools.partial(_dsa_kernel, tq=tq, tk=tk, seq=S, topk=topk, slopes=slopes),
        grid=(B, S // tq),
        in_specs=[qrow(width, col0), seqb(width, col0 + 1), seqb(width, col0 + 2),
                  qrow(iq.shape[-1], 0), seqb(ik.shape[-1], 0), qrow(iw.shape[-1], 0)],
        out_specs=qrow(width, 0),
        out_shape=jax.ShapeDtypeStruct((B, S, width), BF16),
        scratch_shapes=[pltpu.VMEM((S // tk, tq, tk), I32), pltpu.VMEM((S // tk, tq, tk), F32),
                        pltpu.VMEM((n_heads, tq, 2 * DSA_HEAD_DIM), BF16),
                        pltpu.VMEM((n_heads, tq, 2 * DSA_HEAD_DIM), F32)],
        compiler_params=_cparams(2), name="dsa_attn")(main, main, main, iq, ik, iw)


def _outproj_kernel(a_ref, b_ref, h_ref, w_ref, g_ref, beta_ref, o_ref, *, alpha):
    wa = a_ref.shape[1]
    y = jnp.dot(a_ref[...], w_ref[:wa, :], preferred_element_type=F32)
    y = y + jnp.dot(b_ref[...], w_ref[wa:, :], preferred_element_type=F32)
    o_ref[...] = _layer_norm(alpha * h_ref[...] + y, g_ref[...], beta_ref[...])


def _output_projection(a, b, h, w_out, g, beta, alpha, tm=512):
    T, D = h.shape
    rows = lambda n: pl.BlockSpec((tm, n), lambda t: (t, 0))
    vec = pl.BlockSpec((1, D), lambda t: (0, 0))
    return pl.pallas_call(
        functools.partial(_outproj_kernel, alpha=alpha), grid=(T // tm,),
        in_specs=[rows(a.shape[1]), rows(b.shape[1]), rows(D),
                  pl.BlockSpec(w_out.shape, lambda t: (0, 0)), vec, vec],
        out_specs=rows(D), out_shape=jax.ShapeDtypeStruct((T, D), F32),
        compiler_params=_cparams(1), name="out_proj_ln1")(
            a, b, h, w_out, g.reshape(1, D), beta.reshape(1, D))


def _top_rows(x, n):
    rows = []
    for _ in range(n):
        mx = jnp.max(x, axis=0, keepdims=True)
        rows.append(mx)
        x = jnp.where(x == mx, -jnp.inf, x)
    return rows


def _route_kernel(h_ref, wq_ref, sk_ref, thr_ref, c1_ref, s2_ref, e2_ref):
    hb = h_ref[...].astype(BF16)
    qd = 2 * PEER_HALF_DIM
    for hd in range(PEER_HEADS):
        q = jnp.dot(hb, wq_ref[:, hd * qd:(hd + 1) * qd], preferred_element_type=F32).astype(BF16)
        s1 = _dot_nt(sk_ref[hd, 0], q[:, :PEER_HALF_DIM])
        s2 = _dot_nt(sk_ref[hd, 1], q[:, PEER_HALF_DIM:])
        top1 = _top_rows(s1, PEER_TOPK)
        top2 = jnp.concatenate(_top_rows(s2, PEER_TOPK), axis=0)
        cand = jnp.concatenate([r + top2 for r in top1], axis=0)
        best = _top_rows(cand, PEER_TOPK + 1)
        z = jnp.ones_like(best[0])
        for r in best[1:PEER_TOPK]:
            z = z + jnp.exp(r - best[0])
        cut = 0.5 * (best[PEER_TOPK - 1] + best[PEER_TOPK])
        thr_ref[hd] = cut - s1
        c1_ref[hd] = jnp.where(s1 >= top1[-1], jnp.exp(s1 - top1[0]) / z, 0.0)
        s2_ref[hd] = s2
        e2_ref[hd] = jnp.where(s2 >= top2[PEER_TOPK - 1:], jnp.exp(s2 - top2[0:1]), 0.0)


def _peer_route(h, wq, subkeys, tm=256):
    T, D = h.shape
    out = jax.ShapeDtypeStruct((PEER_HEADS, PEER_N_KEYS, T), F32)
    ospec = pl.BlockSpec((PEER_HEADS, PEER_N_KEYS, tm), lambda t: (0, 0, t))
    return pl.pallas_call(
        _route_kernel, grid=(T // tm,),
        in_specs=[pl.BlockSpec((tm, D), lambda t: (t, 0)),
                  pl.BlockSpec(wq.shape, lambda t: (0, 0)),
                  pl.BlockSpec(subkeys.shape, lambda t: (0, 0, 0, 0))],
        out_specs=[ospec] * 4, out_shape=[out] * 4,
        compiler_params=_cparams(1), name="peer_route")(h, wq, subkeys)


def _gelu(x):
    return 0.5 * x * (1.0 + lax.erf(x * (2.0 ** -0.5)))


def _peer_kernel(h_ref, u_ref, vt_ref, thr_ref, c1_ref, s2_ref, e2_ref, o_ref, xb_sc, acc_sc, *, te, sub):
    j = pl.program_id(1)

    @pl.when(j == 0)
    def _():
        xb_sc[...] = h_ref[...].astype(BF16)
        acc_sc[...] = jnp.zeros_like(acc_sc)

    for sb in range(te // sub):
        rows = slice(sb * sub, (sb + 1) * sub)
        act = _gelu(_dot_nt(u_ref[rows, :], xb_sc[...]))
        gates = []
        for ii in range(sub // PEER_N_KEYS):
            i = (j * te + sb * sub) // PEER_N_KEYS + ii
            gate = jnp.zeros((PEER_N_KEYS, act.shape[1]), F32)
            for hd in range(PEER_HEADS):
                thr = thr_ref[hd, pl.ds(i, 1), :]
                c1 = c1_ref[hd, pl.ds(i, 1), :]
                gate = gate + jnp.where(s2_ref[hd] >= thr, e2_ref[hd] * c1, 0.0)
            gates.append(gate)
        w = (jnp.concatenate(gates, axis=0) * act).astype(BF16)
        acc_sc[...] += jnp.dot(vt_ref[:, rows], w, preferred_element_type=F32)

    @pl.when(j == pl.num_programs(1) - 1)
    def _():
        o_ref[...] = acc_sc[...].T


def _peer_experts(h, u_bf, vt_bf, thr, c1, s2, e2, tm=512, te=1024, sub=256):
    T, D = h.shape
    E = u_bf.shape[0]
    rspec = pl.BlockSpec((PEER_HEADS, PEER_N_KEYS, tm), lambda t, j: (0, 0, t))
    return pl.pallas_call(
        functools.partial(_peer_kernel, te=te, sub=sub), grid=(T // tm, E // te),
        in_specs=[pl.BlockSpec((tm, D), lambda t, j: (t, 0)),
                  pl.BlockSpec((te, D), lambda t, j: (j, 0)),
                  pl.BlockSpec((D, te), lambda t, j: (0, j)),
                  rspec, rspec, rspec, rspec],
        out_specs=pl.BlockSpec((tm, D), lambda t, j: (t, 0)),
        out_shape=jax.ShapeDtypeStruct((T, D), F32),
        scratch_shapes=[pltpu.VMEM((tm, D), BF16), pltpu.VMEM((D, tm), F32)],
        compiler_params=_cparams(2), name="peer_experts")(h, u_bf, vt_bf, thr, c1, s2, e2)


def _ple_kernel(h_ref, f_ref, p_ref, wg_ref, bg_ref, wp_ref, g_ref, beta_ref, o_ref, *, alpha):
    r = alpha * h_ref[...] + f_ref[...]
    z = jnp.dot(r.astype(BF16), wg_ref[...], preferred_element_type=F32) + bg_ref[...]
    gate = 1.0 / (1.0 + jnp.exp(-z))
    r = r + gate * jnp.dot(p_ref[...].astype(BF16), wp_ref[...], preferred_element_type=F32)
    o_ref[...] = _layer_norm(r, g_ref[...], beta_ref[...])


def _ple_ln2(h, f, p, layer, wg, bg, wp, g, beta, alpha, tm=512):
    T, D = h.shape
    pd = p.shape[-1]
    rows = pl.BlockSpec((tm, D), lambda t: (t, 0))
    vec = pl.BlockSpec((1, D), lambda t: (0, 0))
    return pl.pallas_call(
        functools.partial(_ple_kernel, alpha=alpha), grid=(T // tm,),
        in_specs=[rows, rows, pl.BlockSpec((None, tm, pd), lambda t: (layer, t, 0)),
                  pl.BlockSpec(wg.shape, lambda t: (0, 0)), vec,
                  pl.BlockSpec(wp.shape, lambda t: (0, 0)), vec, vec],
        out_specs=rows, out_shape=jax.ShapeDtypeStruct((T, D), F32),
        compiler_params=_cparams(1), name="ple_ln2")(
            h, f, p, wg, bg.reshape(1, D), wp, g.reshape(1, D), beta.reshape(1, D))


def _mixer_layer(h, B, S, w_in, w_out, lam_vecs, subln_g, lam_init, ln_g, ln_b, alpha):
    T, D = h.shape
    n_diff = (D // 2) // (2 * DIFF_HEAD_DIM)
    n_dsa = (D - D // 2) // DSA_HEAD_DIM
    n_main = 3 * (D // 2) + 3 * n_dsa * DSA_HEAD_DIM
    n_iq = IDX_HEADS * IDX_DIM
    wb = w_in.astype(BF16)
    w_main = wb[:, :n_main]
    w_iq = wb[:, n_main:n_main + n_iq]
    w_ik = wb[:, n_main + n_iq:n_main + n_iq + IDX_DIM]
    w_iw = jnp.pad(wb[:, n_main + n_iq + IDX_DIM:], ((0, 0), (0, LANES - IDX_HEADS)))
    main, iq, ik, iw = _input_projection(h, w_main, w_iq, w_ik, w_iw)
    main = main.reshape(B, S, n_main)
    a = _diff_attention(main, lam_vecs, subln_g, lam_init, n_diff, 0)
    b = _dsa_attention(main, iq.reshape(B, S, -1), ik.reshape(B, S, -1), iw.reshape(B, S, -1), n_dsa, 3)
    return _output_projection(a.reshape(T, -1), b.reshape(T, -1), h, w_out.astype(BF16), ln_g, ln_b, alpha)


def kernel(x, p, ln_in_g, ln_in_b, w_in, w_out, diff_lambda, diff_subln_g, ln1_g, ln1_b, peer_wq, peer_subkeys, peer_u, peer_v, ple_wg, ple_bg, ple_wp, ln2_g, ln2_b):
    B, S, D = x.shape
    depth = w_in.shape[0]
    T = B * S
    alpha = (2 * depth) ** 0.25
    h = _entry_layer_norm(x.reshape(T, D), ln_in_g, ln_in_b)
    p2 = p.reshape(depth, T, p.shape[-1])
    for i in range(depth):
        lam_init = 0.8 - 0.6 * math.exp(-0.3 * i)
        h = _mixer_layer(h, B, S, w_in[i], w_out[i], diff_lambda[i], diff_subln_g[i], lam_init,
                         ln1_g[i], ln1_b[i], alpha)
        thr, c1, s2, e2 = _peer_route(h, peer_wq[i].astype(BF16), peer_subkeys[i].astype(BF16))
        f = _peer_experts(h, peer_u[i].astype(BF16), peer_v[i].astype(BF16).T, thr, c1, s2, e2)
        h = _ple_ln2(h, f, p2, i, ple_wg[i].astype(BF16), ple_bg[i], ple_wp[i].astype(BF16),
                     ln2_g[i], ln2_b[i], alpha)
    return h.reshape(B, S, D)
```

```python
import functools
import math
import struct

import jax
import jax.numpy as jnp
from jax import lax
from jax.experimental import pallas as pl
from jax.experimental.pallas import tpu as pltpu

F32 = jnp.float32
BF16 = jnp.bfloat16
I32 = jnp.int32

LN_EPS = 1e-5
NEG_INF = -1e30
CHUNK = 64
CHUNK_SHIFT = 6
LANES = 128
VMEM_LIMIT = 56 * 1024 * 1024

DIFF_HEAD_DIM = 64
DSA_HEAD_DIM = 64
IDX_HEADS = 4
IDX_DIM = 64
IDX_TOPK_MAX = 256
PEER_HEADS = 8
PEER_N_KEYS = 128
PEER_HALF_DIM = 128
PEER_TOPK = 16

_NEG_BITS = struct.unpack("<i", struct.pack("<f", NEG_INF))[0]
KEY_NEG = _NEG_BITS ^ 0x7FFFFFFF


def _cparams(n_axes):
    return pltpu.CompilerParams(
        dimension_semantics=("arbitrary",) * n_axes, vmem_limit_bytes=VMEM_LIMIT)


def _dot_nt(a, b):
    return lax.dot_general(a, b, (((1,), (1,)), ((), ())), preferred_element_type=F32)


def _layer_norm(x, g, b):
    mu = jnp.mean(x, axis=-1, keepdims=True)
    xc = x - mu
    var = jnp.mean(xc * xc, axis=-1, keepdims=True)
    return xc * lax.rsqrt(var + LN_EPS) * g + b


def _ln_kernel(x_ref, g_ref, b_ref, o_ref):
    o_ref[...] = _layer_norm(x_ref[...], g_ref[...], b_ref[...])


def _entry_layer_norm(x, g, b, tm=512):
    T, D = x.shape
    row = pl.BlockSpec((tm, D), lambda t: (t, 0))
    vec = pl.BlockSpec((1, D), lambda t: (0, 0))
    return pl.pallas_call(
        _ln_kernel, grid=(T // tm,), in_specs=[row, vec, vec], out_specs=row,
        out_shape=jax.ShapeDtypeStruct((T, D), F32), compiler_params=_cparams(1),
        name="entry_ln")(x, g.reshape(1, D), b.reshape(1, D))


def _inproj_kernel(h_ref, wm_ref, wvt_ref, wiq_ref, wik_ref, wiwt_ref, main_ref, vt_ref, iq_ref, ik_ref, iwt_ref, *, tn):
    hb = h_ref[...].astype(BF16)
    for j in range(0, wm_ref.shape[1], tn):
        main_ref[:, j:j + tn] = jnp.dot(hb, wm_ref[:, j:j + tn], preferred_element_type=F32).astype(BF16)
    vt_ref[0] = _dot_nt(wvt_ref[...], hb).astype(BF16)
    iq_ref[...] = jnp.dot(hb, wiq_ref[...], preferred_element_type=F32) * (IDX_DIM ** -0.5)
    ik_ref[...] = jnp.dot(hb, wik_ref[...], preferred_element_type=F32)
    iwt_ref[...] = _dot_nt(wiwt_ref[...], hb) * (IDX_HEADS ** -0.5)


def _input_projection(h, w_main, w_vt, w_iq, w_ik, w_iwt, tm=512, tn=512):
    T, D = h.shape
    nm = w_main.shape[1]
    nv = w_vt.shape[0]

    def full(w):
        return pl.BlockSpec(w.shape, lambda t: (0, 0))

    def rows(n):
        return pl.BlockSpec((tm, n), lambda t: (t, 0))

    return pl.pallas_call(
        functools.partial(_inproj_kernel, tn=tn), grid=(T // tm,),
        in_specs=[rows(D), full(w_main), full(w_vt), full(w_iq), full(w_ik), full(w_iwt)],
        out_specs=[rows(nm), pl.BlockSpec((1, nv, tm), lambda t: (t, 0, 0)),
                   rows(w_iq.shape[1]), rows(w_ik.shape[1]),
                   pl.BlockSpec((w_iwt.shape[0], tm), lambda t: (0, t))],
        out_shape=[jax.ShapeDtypeStruct((T, nm), BF16),
                   jax.ShapeDtypeStruct((T // tm, nv, tm), BF16),
                   jax.ShapeDtypeStruct((T, w_iq.shape[1]), F32),
                   jax.ShapeDtypeStruct((T, w_ik.shape[1]), F32),
                   jax.ShapeDtypeStruct((w_iwt.shape[0], T), F32)],
        compiler_params=_cparams(1), name="in_proj")(h, w_main, w_vt, w_iq, w_ik, w_iwt)


def _diff_kernel(lamv_ref, g_ref, q_ref, k_ref, v_ref, o_ref, qm_sc, acc_sc, *, tq, lam_init, slopes):
    qi = pl.program_id(1)
    tk = tq
    dv = 2 * DIFF_HEAD_DIM
    n_heads = len(slopes)
    lane = lax.broadcasted_iota(I32, (tq, dv), 1)
    for hd in range(n_heads):
        qs = q_ref[0, :, hd * dv:(hd + 1) * dv] * (DIFF_HEAD_DIM ** -0.5)
        zero = jnp.zeros_like(qs)
        qm_sc[2 * hd] = jnp.where(lane < DIFF_HEAD_DIM, qs, zero)
        qm_sc[2 * hd + 1] = jnp.where(lane >= DIFF_HEAD_DIM, qs, zero)
    acc_sc[...] = jnp.zeros_like(acc_sc)
    t_idx = qi * tq + lax.broadcasted_iota(I32, (tq, tk), 0)

    def step(kb, carry, diagonal):
        off = pl.multiple_of(kb * tk, tk)
        s_idx = off + lax.broadcasted_iota(I32, (tq, tk), 1)
        dist = jnp.abs(t_idx - s_idx).astype(F32)
        if diagonal:
            visible = s_idx < (((t_idx >> CHUNK_SHIFT) + 1) << CHUNK_SHIFT)
        out = []
        for hd in range(n_heads):
            k = k_ref[0, pl.ds(off, tk), hd * dv:(hd + 1) * dv]
            v = v_ref[0, pl.ds(off, tk), hd * dv:(hd + 1) * dv]
            bias = -slopes[hd] * dist
            for mp in range(2):
                c = 2 * hd + mp
                s = _dot_nt(qm_sc[c], k) + bias
                if diagonal:
                    s = jnp.where(visible, s, NEG_INF)
                m, l = carry[2 * c], carry[2 * c + 1]
                m_new = jnp.maximum(m, jnp.max(s, axis=-1, keepdims=True))
                p = jnp.exp(s - m_new)
                a = jnp.exp(m - m_new)
                acc_sc[c] = a * acc_sc[c] + jnp.dot(p.astype(BF16), v, preferred_element_type=F32)
                out.extend((m_new, a * l + jnp.sum(p, axis=-1, keepdims=True)))
        return tuple(out)

    m0 = jnp.full((tq, 1), -jnp.inf, F32)
    l0 = jnp.zeros((tq, 1), F32)
    carry = lax.fori_loop(0, qi, functools.partial(step, diagonal=False), (m0, l0) * (2 * n_heads))
    carry = step(qi, carry, True)

    lv = lamv_ref[...]
    lam = (jnp.exp(jnp.sum(lv[0:1] * lv[1:2], axis=-1, keepdims=True))
           - jnp.exp(jnp.sum(lv[2:3] * lv[3:4], axis=-1, keepdims=True)) + lam_init)
    for hd in range(n_heads):
        o = acc_sc[2 * hd] / carry[4 * hd + 1] - lam * (acc_sc[2 * hd + 1] / carry[4 * hd + 3])
        o = o * lax.rsqrt(jnp.mean(o * o, axis=-1, keepdims=True) + LN_EPS) * g_ref[...]
        o_ref[0, :, hd * dv:(hd + 1) * dv] = (o * (1.0 - lam_init)).astype(o_ref.dtype)


def _diff_attention(main, lam_vecs, subln_g, lam_init, n_heads, col0, tq=256):
    B, S, _ = main.shape
    dv = 2 * DIFF_HEAD_DIM
    width = n_heads * dv
    slopes = tuple(2.0 ** (-8.0 * (i + 1) / n_heads) for i in range(n_heads))
    kv = lambda c: pl.BlockSpec((1, S, width), lambda b, q: (b, 0, c))
    qo = lambda c: pl.BlockSpec((1, tq, width), lambda b, q: (b, q, c))
    return pl.pallas_call(
        functools.partial(_diff_kernel, tq=tq, lam_init=lam_init, slopes=slopes),
        grid=(B, S // tq),
        in_specs=[pl.BlockSpec(lam_vecs.shape, lambda b, q: (0, 0)),
                  pl.BlockSpec((1, dv), lambda b, q: (0, 0)),
                  qo(col0), kv(col0 + 1), kv(col0 + 2)],
        out_specs=qo(0),
        out_shape=jax.ShapeDtypeStruct((B, S, width), BF16),
        scratch_shapes=[pltpu.VMEM((2 * n_heads, tq, dv), BF16), pltpu.VMEM((2 * n_heads, tq, dv), F32)],
        compiler_params=_cparams(2), name="diff_attn")(
            lam_vecs, subln_g.reshape(1, dv), main, main, main)


def _sort_key(x):
    bits = pltpu.bitcast(x + 0.0, I32)
    return jnp.where(bits < 0, bits ^ 0x7FFFFFFF, bits)


def _dsa_kernel(q_ref, k_ref, vt_ref, iq_ref, ik_ref, iwt_ref, o_ref, key_sc, mask_sc, qm_sc, acc_sc, *,
                tq, tk, seq, topk, slopes):
    qi = pl.program_id(1)
    q_pos0 = qi * tq
    nkb = (q_pos0 + tq + tk - 1) // tk
    n_tail = seq - nkb * tk
    t_row = q_pos0 + lax.broadcasted_iota(I32, (1, tq), 1)
    chunk_end = ((t_row >> CHUNK_SHIFT) + 1) << CHUNK_SHIFT
    s_col = lax.broadcasted_iota(I32, (tk, tq), 0)

    iq = iq_ref[0].astype(BF16)
    iq_heads = [iq[:, j * IDX_DIM:(j + 1) * IDX_DIM] for j in range(IDX_HEADS)]
    iw_rows = [iwt_ref[j:j + 1, :] for j in range(IDX_HEADS)]

    def score_block(kb, _):
        off = pl.multiple_of(kb * tk, tk)
        ik = ik_ref[0, pl.ds(off, tk), :].astype(BF16)
        acc = jnp.zeros((tk, tq), F32)
        for j in range(IDX_HEADS):
            acc = acc + iw_rows[j] * jnp.maximum(_dot_nt(ik, iq_heads[j]), 0.0)
        key_sc[kb] = _sort_key(jnp.where(off + s_col < chunk_end, acc, NEG_INF))
        return 0

    lax.fori_loop(0, nkb, score_block, 0)

    def count(pred):
        def body(kb, c):
            return c + jnp.sum(pred(key_sc[kb]).astype(I32), axis=0, keepdims=True)
        return lax.fori_loop(0, nkb, body, jnp.zeros((1, tq), I32))

    bits_per_check = 4

    def searching(st):
        i, _, done = st
        return jnp.logical_and(i < 32, jnp.min(done) == 0)

    def bit_steps(st):
        i, r, done = st
        for u in range(bits_per_check):
            cand = jnp.where(done > 0, r, r ^ (jnp.int32(1) << (31 - i - u)))
            cnt = count(lambda kk: kk >= cand) + jnp.where(cand <= KEY_NEG, n_tail, 0)
            r = jnp.where(cnt >= topk, cand, r)
            done = jnp.where(cnt == topk, 1, done)
        return i + bits_per_check, r, done

    _, tau, _ = lax.while_loop(
        searching, bit_steps, (jnp.int32(0), jnp.full((1, tq), -2 ** 31, I32), jnp.zeros((1, tq), I32)))
    cnt_gt = count(lambda kk: kk > tau) + jnp.where(tau < KEY_NEG, n_tail, 0)
    cnt_eq = count(lambda kk: kk == tau) + jnp.where(tau == KEY_NEG, n_tail, 0)
    need = topk - cnt_gt
    has_ties = jnp.max(cnt_eq - need) > 0

    @pl.when(jnp.logical_not(has_ties))
    def _():
        def body(kb, _):
            sel = jnp.logical_and(key_sc[kb] >= tau, kb * tk + s_col < chunk_end)
            mask_sc[kb] = jnp.where(sel, 0.0, NEG_INF)
            return 0
        lax.fori_loop(0, nkb, body, 0)

    @pl.when(has_ties)
    def _():
        r_i = lax.broadcasted_iota(I32, (tk, tk), 0)
        c_i = lax.broadcasted_iota(I32, (tk, tk), 1)
        earlier = jnp.where(c_i < r_i, 1.0, 0.0).astype(BF16)

        def body(kb, seen):
            kk = key_sc[kb]
            eq = kk == tau
            eqf = jnp.where(eq, 1.0, 0.0)
            rank = seen + jnp.dot(earlier, eqf.astype(BF16), preferred_element_type=F32)
            take = jnp.logical_and(eq, rank < need.astype(F32))
            sel = jnp.logical_and(jnp.logical_or(kk > tau, take), kb * tk + s_col < chunk_end)
            mask_sc[kb] = jnp.where(sel, 0.0, NEG_INF)
            return seen + jnp.sum(eqf, axis=0, keepdims=True)
        lax.fori_loop(0, nkb, body, jnp.zeros((1, tq), F32))

    hdim = DSA_HEAD_DIM
    pair = 2 * hdim
    n_heads = len(slopes)
    lane = lax.broadcasted_iota(I32, (tq, pair), 1)
    for p in range(n_heads // 2):
        qs = q_ref[0, :, p * pair:(p + 1) * pair] * (hdim ** -0.5)
        zero = jnp.zeros_like(qs)
        qm_sc[2 * p] = jnp.where(lane < hdim, qs, zero)
        qm_sc[2 * p + 1] = jnp.where(lane >= hdim, qs, zero)
    acc_sc[...] = jnp.zeros_like(acc_sc)

    def attend(kb, carry):
        off = pl.multiple_of(kb * tk, tk)
        dist = jnp.abs(t_row - (off + s_col)).astype(F32)
        madd = mask_sc[kb]
        out = []
        for hd in range(n_heads):
            k = k_ref[0, pl.ds(off, tk), (hd // 2) * pair:(hd // 2 + 1) * pair]
            s = _dot_nt(k, qm_sc[hd]) - slopes[hd] * dist + madd
            m, l = carry[2 * hd], carry[2 * hd + 1]
            m_new = jnp.maximum(m, jnp.max(s, axis=0, keepdims=True))
            pr = jnp.exp(s - m_new)
            a = jnp.exp(m - m_new)
            rows = slice(hd * hdim, (hd + 1) * hdim)
            acc_sc[rows, :] = a * acc_sc[rows, :] + jnp.dot(vt_ref[kb, rows, :], pr.astype(BF16),
                                                            preferred_element_type=F32)
            out.extend((m_new, a * l + jnp.sum(pr, axis=0, keepdims=True)))
        return tuple(out)

    m0 = jnp.full((1, tq), -jnp.inf, F32)
    l0 = jnp.zeros((1, tq), F32)
    carry = lax.fori_loop(0, nkb, attend, (m0, l0) * n_heads)
    for hd in range(n_heads):
        rows = slice(hd * hdim, (hd + 1) * hdim)
        acc_sc[rows, :] = acc_sc[rows, :] / carry[2 * hd + 1]
    o_ref[0] = acc_sc[...].T.astype(o_ref.dtype)


def _dsa_attention(main, vt, iq, ik, iwt, n_heads, col_q, col_k, vrow0, tq=256):
    B, S, _ = main.shape
    tk = vt.shape[-1]
    width = n_heads * DSA_HEAD_DIM
    nq = S // tq
    topk = min(IDX_TOPK_MAX, S // 4)
    slopes = tuple(2.0 ** (-8.0 * (i + 1) / n_heads) for i in range(n_heads))
    qrow = lambda n, c: pl.BlockSpec((1, tq, n), lambda b, q: (b, q, c))
    seqb = lambda n, c: pl.BlockSpec((1, S, n), lambda b, q: (b, 0, c))
    return pl.pallas_call(
        functools.partial(_dsa_kernel, tq=tq, tk=tk, seq=S, topk=topk, slopes=slopes),
        grid=(B, nq),
        in_specs=[qrow(width, col_q), seqb(width, col_k),
                  pl.BlockSpec((S // tk, width, tk), lambda b, q: (b, vrow0, 0)),
                  qrow(iq.shape[-1], 0), seqb(ik.shape[-1], 0),
                  pl.BlockSpec((iwt.shape[0], tq), lambda b, q: (0, b * nq + q))],
        out_specs=qrow(width, 0),
        out_shape=jax.ShapeDtypeStruct((B, S, width), BF16),
        scratch_shapes=[pltpu.VMEM((S // tk, tk, tq), I32), pltpu.VMEM((S // tk, tk, tq), F32),
                        pltpu.VMEM((n_heads, tq, 2 * DSA_HEAD_DIM), BF16),
                        pltpu.VMEM((width, tq), F32)],
        compiler_params=_cparams(2), name="dsa_attn")(main, main, vt, iq, ik, iwt)


def _outproj_kernel(a_ref, b_ref, h_ref, w_ref, g_ref, beta_ref, o_ref, *, alpha):
    wa = a_ref.shape[1]
    y = jnp.dot(a_ref[...], w_ref[:wa, :], preferred_element_type=F32)
    y = y + jnp.dot(b_ref[...], w_ref[wa:, :], preferred_element_type=F32)
    o_ref[...] = _layer_norm(alpha * h_ref[...] + y, g_ref[...], beta_ref[...])


def _output_projection(a, b, h, w_out, g, beta, alpha, tm=512):
    T, D = h.shape
    rows = lambda n: pl.BlockSpec((tm, n), lambda t: (t, 0))
    vec = pl.BlockSpec((1, D), lambda t: (0, 0))
    return pl.pallas_call(
        functools.partial(_outproj_kernel, alpha=alpha), grid=(T // tm,),
        in_specs=[rows(a.shape[1]), rows(b.shape[1]), rows(D),
                  pl.BlockSpec(w_out.shape, lambda t: (0, 0)), vec, vec],
        out_specs=rows(D), out_shape=jax.ShapeDtypeStruct((T, D), F32),
        compiler_params=_cparams(1), name="out_proj_ln1")(
            a, b, h, w_out, g.reshape(1, D), beta.reshape(1, D))


def _top_rows(x, n):
    rows = []
    for _ in range(n):
        mx = jnp.max(x, axis=0, keepdims=True)
        rows.append(mx)
        x = jnp.where(x == mx, -jnp.inf, x)
    return rows


def _route_kernel(h_ref, wq_ref, sk_ref, thr_ref, c1_ref, s2_ref, e2_ref):
    hb = h_ref[...].astype(BF16)
    qd = 2 * PEER_HALF_DIM
    for hd in range(PEER_HEADS):
        q = jnp.dot(hb, wq_ref[:, hd * qd:(hd + 1) * qd], preferred_element_type=F32).astype(BF16)
        s1 = _dot_nt(sk_ref[hd, 0], q[:, :PEER_HALF_DIM])
        s2 = _dot_nt(sk_ref[hd, 1], q[:, PEER_HALF_DIM:])
        top1 = _top_rows(s1, PEER_TOPK)
        top2 = jnp.concatenate(_top_rows(s2, PEER_TOPK), axis=0)
        cand = jnp.concatenate([r + top2 for r in top1], axis=0)
        best = _top_rows(cand, PEER_TOPK + 1)
        z = jnp.ones_like(best[0])
        for r in best[1:PEER_TOPK]:
            z = z + jnp.exp(r - best[0])
        cut = 0.5 * (best[PEER_TOPK - 1] + best[PEER_TOPK])
        thr_ref[hd] = cut - s1
        c1_ref[hd] = jnp.where(s1 >= top1[-1], jnp.exp(s1 - top1[0]) / z, 0.0)
        s2_ref[hd] = s2
        e2_ref[hd] = jnp.where(s2 >= top2[PEER_TOPK - 1:], jnp.exp(s2 - top2[0:1]), 0.0)


def _peer_route(h, wq, subkeys, tm=256):
    T, D = h.shape
    out = jax.ShapeDtypeStruct((PEER_HEADS, PEER_N_KEYS, T), F32)
    ospec = pl.BlockSpec((PEER_HEADS, PEER_N_KEYS, tm), lambda t: (0, 0, t))
    return pl.pallas_call(
        _route_kernel, grid=(T // tm,),
        in_specs=[pl.BlockSpec((tm, D), lambda t: (t, 0)),
                  pl.BlockSpec(wq.shape, lambda t: (0, 0)),
                  pl.BlockSpec(subkeys.shape, lambda t: (0, 0, 0, 0))],
        out_specs=[ospec] * 4, out_shape=[out] * 4,
        compiler_params=_cparams(1), name="peer_route")(h, wq, subkeys)


def _gelu(x):
    return 0.5 * x * (1.0 + lax.erf(x * (2.0 ** -0.5)))


def _peer_kernel(h_ref, u_ref, vt_ref, thr_ref, c1_ref, s2_ref, e2_ref, o_ref, xb_sc, acc_sc, *, te, sub):
    j = pl.program_id(1)

    @pl.when(j == 0)
    def _():
        xb_sc[...] = h_ref[...].astype(BF16)
        acc_sc[...] = jnp.zeros_like(acc_sc)

    for sb in range(te // sub):
        rows = slice(sb * sub, (sb + 1) * sub)
        act = _gelu(_dot_nt(u_ref[rows, :], xb_sc[...]))
        gates = []
        for ii in range(sub // PEER_N_KEYS):
            i = (j * te + sb * sub) // PEER_N_KEYS + ii
            gate = jnp.zeros((PEER_N_KEYS, act.shape[1]), F32)
            for hd in range(PEER_HEADS):
                thr = thr_ref[hd, pl.ds(i, 1), :]
                c1 = c1_ref[hd, pl.ds(i, 1), :]
                gate = gate + jnp.where(s2_ref[hd] >= thr, e2_ref[hd] * c1, 0.0)
            gates.append(gate)
        w = (jnp.concatenate(gates, axis=0) * act).astype(BF16)
        acc_sc[...] += jnp.dot(vt_ref[:, rows], w, preferred_element_type=F32)

    @pl.when(j == pl.num_programs(1) - 1)
    def _():
        o_ref[...] = acc_sc[...].T


def _peer_experts(h, u_bf, vt_bf, thr, c1, s2, e2, tm=512, te=1024, sub=256):
    T, D = h.shape
    E = u_bf.shape[0]
    rspec = pl.BlockSpec((PEER_HEADS, PEER_N_KEYS, tm), lambda t, j: (0, 0, t))
    return pl.pallas_call(
        functools.partial(_peer_kernel, te=te, sub=sub), grid=(T // tm, E // te),
        in_specs=[pl.BlockSpec((tm, D), lambda t, j: (t, 0)),
                  pl.BlockSpec((te, D), lambda t, j: (j, 0)),
                  pl.BlockSpec((D, te), lambda t, j: (0, j)),
                  rspec, rspec, rspec, rspec],
        out_specs=pl.BlockSpec((tm, D), lambda t, j: (t, 0)),
        out_shape=jax.ShapeDtypeStruct((T, D), F32),
        scratch_shapes=[pltpu.VMEM((tm, D), BF16), pltpu.VMEM((D, tm), F32)],
        compiler_params=_cparams(2), name="peer_experts")(h, u_bf, vt_bf, thr, c1, s2, e2)


def _ple_kernel(h_ref, f_ref, p_ref, wg_ref, bg_ref, wp_ref, g_ref, beta_ref, o_ref, *, alpha):
    r = alpha * h_ref[...] + f_ref[...]
    z = jnp.dot(r.astype(BF16), wg_ref[...], preferred_element_type=F32) + bg_ref[...]
    gate = 1.0 / (1.0 + jnp.exp(-z))
    r = r + gate * jnp.dot(p_ref[...].astype(BF16), wp_ref[...], preferred_element_type=F32)
    o_ref[...] = _layer_norm(r, g_ref[...], beta_ref[...])


def _ple_ln2(h, f, p, layer, wg, bg, wp, g, beta, alpha, tm=512):
    T, D = h.shape
    pd = p.shape[-1]
    rows = pl.BlockSpec((tm, D), lambda t: (t, 0))
    vec = pl.BlockSpec((1, D), lambda t: (0, 0))
    return pl.pallas_call(
        functools.partial(_ple_kernel, alpha=alpha), grid=(T // tm,),
        in_specs=[rows, rows, pl.BlockSpec((None, tm, pd), lambda t: (layer, t, 0)),
                  pl.BlockSpec(wg.shape, lambda t: (0, 0)), vec,
                  pl.BlockSpec(wp.shape, lambda t: (0, 0)), vec, vec],
        out_specs=rows, out_shape=jax.ShapeDtypeStruct((T, D), F32),
        compiler_params=_cparams(1), name="ple_ln2")(
            h, f, p, wg, bg.reshape(1, D), wp, g.reshape(1, D), beta.reshape(1, D))


def _mixer_layer(h, B, S, w_in, w_out, lam_vecs, subln_g, lam_init, ln_g, ln_b, alpha):
    T, D = h.shape
    dw = D // 2
    n_diff = dw // (2 * DIFF_HEAD_DIM)
    sw = D - dw
    n_dsa = sw // DSA_HEAD_DIM
    n_iq = IDX_HEADS * IDX_DIM
    wb = w_in.astype(BF16)
    c_sv = 3 * dw + 2 * sw
    c_iq = c_sv + sw
    w_main = wb[:, :c_sv]
    w_vt = wb[:, c_sv:c_iq].T
    w_iq = wb[:, c_iq:c_iq + n_iq]
    w_ik = wb[:, c_iq + n_iq:c_iq + n_iq + IDX_DIM]
    w_iwt = jnp.pad(wb[:, c_iq + n_iq + IDX_DIM:].T, ((0, 8 - IDX_HEADS), (0, 0)))
    main, vt, iq, ik, iwt = _input_projection(h, w_main, w_vt, w_iq, w_ik, w_iwt)
    main = main.reshape(B, S, c_sv)
    a = _diff_attention(main, lam_vecs, subln_g, lam_init, n_diff, 0)
    b = _dsa_attention(main, vt, iq.reshape(B, S, -1), ik.reshape(B, S, -1), iwt, n_dsa,
                       3 * dw // sw, 3 * dw // sw + 1, 0)
    return _output_projection(a.reshape(T, -1), b.reshape(T, -1), h, w_out.astype(BF16), ln_g, ln_b, alpha)


def kernel(x, p, ln_in_g, ln_in_b, w_in, w_out, diff_lambda, diff_subln_g, ln1_g, ln1_b, peer_wq, peer_subkeys, peer_u, peer_v, ple_wg, ple_bg, ple_wp, ln2_g, ln2_b):
    B, S, D = x.shape
    depth = w_in.shape[0]
    T = B * S
    alpha = (2 * depth) ** 0.25
    h = _entry_layer_norm(x.reshape(T, D), ln_in_g, ln_in_b)
    p2 = p.reshape(depth, T, p.shape[-1])
    for i in range(depth):
        lam_init = 0.8 - 0.6 * math.exp(-0.3 * i)
        h = _mixer_layer(h, B, S, w_in[i], w_out[i], diff_lambda[i], diff_subln_g[i], lam_init,
                         ln1_g[i], ln1_b[i], alpha)
        thr, c1, s2, e2 = _peer_route(h, peer_wq[i].astype(BF16), peer_subkeys[i].astype(BF16))
        f = _peer_experts(h, peer_u[i].astype(BF16), peer_v[i].astype(BF16).T, thr, c1, s2, e2)
        h = _ple_ln2(h, f, p2, i, ple_wg[i].astype(BF16), ple_bg[i], ple_wp[i].astype(BF16),
                     ln2_g[i], ln2_b[i], alpha)
    return h.reshape(B, S, D)
```

```python
import functools
import math
import struct

import jax
import jax.numpy as jnp
from jax import lax
from jax.experimental import pallas as pl
from jax.experimental.pallas import tpu as pltpu

F32 = jnp.float32
BF16 = jnp.bfloat16
I32 = jnp.int32

LN_EPS = 1e-5
NEG_INF = -1e30
CHUNK = 64
CHUNK_SHIFT = 6
LANES = 128
VMEM_LIMIT = 56 * 1024 * 1024

DIFF_HEAD_DIM = 64
DSA_HEAD_DIM = 64
IDX_HEADS = 4
IDX_DIM = 64
IDX_TOPK_MAX = 256
PEER_HEADS = 8
PEER_N_KEYS = 128
PEER_HALF_DIM = 128
PEER_TOPK = 16

_NEG_BITS = struct.unpack("<i", struct.pack("<f", NEG_INF))[0]
KEY_NEG = _NEG_BITS ^ 0x7FFFFFFF


def _cparams(n_axes):
    return pltpu.CompilerParams(
        dimension_semantics=("arbitrary",) * n_axes, vmem_limit_bytes=VMEM_LIMIT)


def _dot_nt(a, b):
    return lax.dot_general(a, b, (((1,), (1,)), ((), ())), preferred_element_type=F32)


def _layer_norm(x, g, b):
    mu = jnp.mean(x, axis=-1, keepdims=True)
    xc = x - mu
    var = jnp.mean(xc * xc, axis=-1, keepdims=True)
    return xc * lax.rsqrt(var + LN_EPS) * g + b


def _ln_kernel(x_ref, g_ref, b_ref, o_ref):
    o_ref[...] = _layer_norm(x_ref[...], g_ref[...], b_ref[...])


def _entry_layer_norm(x, g, b, tm=512):
    T, D = x.shape
    row = pl.BlockSpec((tm, D), lambda t: (t, 0))
    vec = pl.BlockSpec((1, D), lambda t: (0, 0))
    return pl.pallas_call(
        _ln_kernel, grid=(T // tm,), in_specs=[row, vec, vec], out_specs=row,
        out_shape=jax.ShapeDtypeStruct((T, D), F32), compiler_params=_cparams(1),
        name="entry_ln")(x, g.reshape(1, D), b.reshape(1, D))


def _inproj_kernel(h_ref, wm_ref, wvt_ref, wiq_ref, wik_ref, wiwt_ref, main_ref, vt_ref, iq_ref, ik_ref, iwt_ref, *, tn):
    hb = h_ref[...].astype(BF16)
    for j in range(0, wm_ref.shape[1], tn):
        main_ref[:, j:j + tn] = jnp.dot(hb, wm_ref[:, j:j + tn], preferred_element_type=F32).astype(BF16)
    vt_ref[0] = _dot_nt(wvt_ref[...], hb).astype(BF16)
    iq_ref[...] = jnp.dot(hb, wiq_ref[...], preferred_element_type=F32) * (IDX_DIM ** -0.5)
    ik_ref[...] = jnp.dot(hb, wik_ref[...], preferred_element_type=F32)
    iwt_ref[...] = _dot_nt(wiwt_ref[...], hb) * (IDX_HEADS ** -0.5)


def _input_projection(h, w_main, w_vt, w_iq, w_ik, w_iwt, tm=512, tn=512):
    T, D = h.shape
    nm = w_main.shape[1]
    nv = w_vt.shape[0]

    def full(w):
        return pl.BlockSpec(w.shape, lambda t: (0, 0))

    def rows(n):
        return pl.BlockSpec((tm, n), lambda t: (t, 0))

    return pl.pallas_call(
        functools.partial(_inproj_kernel, tn=tn), grid=(T // tm,),
        in_specs=[rows(D), full(w_main), full(w_vt), full(w_iq), full(w_ik), full(w_iwt)],
        out_specs=[rows(nm), pl.BlockSpec((1, nv, tm), lambda t: (t, 0, 0)),
                   rows(w_iq.shape[1]), rows(w_ik.shape[1]),
                   pl.BlockSpec((w_iwt.shape[0], tm), lambda t: (0, t))],
        out_shape=[jax.ShapeDtypeStruct((T, nm), BF16),
                   jax.ShapeDtypeStruct((T // tm, nv, tm), BF16),
                   jax.ShapeDtypeStruct((T, w_iq.shape[1]), F32),
                   jax.ShapeDtypeStruct((T, w_ik.shape[1]), F32),
                   jax.ShapeDtypeStruct((w_iwt.shape[0], T), F32)],
        compiler_params=_cparams(1), name="in_proj")(h, w_main, w_vt, w_iq, w_ik, w_iwt)


def _diff_kernel(lamv_ref, g_ref, q_ref, k_ref, v_ref, o_ref, qm_sc, acc_sc, *, tq, lam_init, slopes):
    qi = pl.program_id(1)
    tk = tq
    dv = 2 * DIFF_HEAD_DIM
    n_heads = len(slopes)
    lane = lax.broadcasted_iota(I32, (tq, dv), 1)
    for hd in range(n_heads):
        qs = q_ref[0, :, hd * dv:(hd + 1) * dv] * (DIFF_HEAD_DIM ** -0.5)
        zero = jnp.zeros_like(qs)
        qm_sc[2 * hd] = jnp.where(lane < DIFF_HEAD_DIM, qs, zero)
        qm_sc[2 * hd + 1] = jnp.where(lane >= DIFF_HEAD_DIM, qs, zero)
    acc_sc[...] = jnp.zeros_like(acc_sc)
    t_idx = qi * tq + lax.broadcasted_iota(I32, (tq, tk), 0)

    def step(kb, carry, diagonal):
        off = pl.multiple_of(kb * tk, tk)
        s_idx = off + lax.broadcasted_iota(I32, (tq, tk), 1)
        dist = jnp.abs(t_idx - s_idx).astype(F32)
        if diagonal:
            visible = s_idx < (((t_idx >> CHUNK_SHIFT) + 1) << CHUNK_SHIFT)
        out = []
        for hd in range(n_heads):
            k = k_ref[0, pl.ds(off, tk), hd * dv:(hd + 1) * dv]
            v = v_ref[0, pl.ds(off, tk), hd * dv:(hd + 1) * dv]
            bias = -slopes[hd] * dist
            for mp in range(2):
                c = 2 * hd + mp
                s = _dot_nt(qm_sc[c], k) + bias
                if diagonal:
                    s = jnp.where(visible, s, NEG_INF)
                m, l = carry[2 * c], carry[2 * c + 1]
                m_new = jnp.maximum(m, jnp.max(s, axis=-1, keepdims=True))
                p = jnp.exp(s - m_new)
                a = jnp.exp(m - m_new)
                acc_sc[c] = a * acc_sc[c] + jnp.dot(p.astype(BF16), v, preferred_element_type=F32)
                out.extend((m_new, a * l + jnp.sum(p, axis=-1, keepdims=True)))
        return tuple(out)

    m0 = jnp.full((tq, 1), -jnp.inf, F32)
    l0 = jnp.zeros((tq, 1), F32)
    carry = lax.fori_loop(0, qi, functools.partial(step, diagonal=False), (m0, l0) * (2 * n_heads))
    carry = step(qi, carry, True)

    lv = lamv_ref[...]
    lam = (jnp.exp(jnp.sum(lv[0:1] * lv[1:2], axis=-1, keepdims=True))
           - jnp.exp(jnp.sum(lv[2:3] * lv[3:4], axis=-1, keepdims=True)) + lam_init)
    for hd in range(n_heads):
        o = acc_sc[2 * hd] / carry[4 * hd + 1] - lam * (acc_sc[2 * hd + 1] / carry[4 * hd + 3])
        o = o * lax.rsqrt(jnp.mean(o * o, axis=-1, keepdims=True) + LN_EPS) * g_ref[...]
        o_ref[0, :, hd * dv:(hd + 1) * dv] = (o * (1.0 - lam_init)).astype(o_ref.dtype)


def _diff_attention(main, lam_vecs, subln_g, lam_init, n_heads, col0, tq=256):
    B, S, _ = main.shape
    dv = 2 * DIFF_HEAD_DIM
    width = n_heads * dv
    slopes = tuple(2.0 ** (-8.0 * (i + 1) / n_heads) for i in range(n_heads))
    kv = lambda c: pl.BlockSpec((1, S, width), lambda b, q: (b, 0, c))
    qo = lambda c: pl.BlockSpec((1, tq, width), lambda b, q: (b, q, c))
    return pl.pallas_call(
        functools.partial(_diff_kernel, tq=tq, lam_init=lam_init, slopes=slopes),
        grid=(B, S // tq),
        in_specs=[pl.BlockSpec(lam_vecs.shape, lambda b, q: (0, 0)),
                  pl.BlockSpec((1, dv), lambda b, q: (0, 0)),
                  qo(col0), kv(col0 + 1), kv(col0 + 2)],
        out_specs=qo(0),
        out_shape=jax.ShapeDtypeStruct((B, S, width), BF16),
        scratch_shapes=[pltpu.VMEM((2 * n_heads, tq, dv), BF16), pltpu.VMEM((2 * n_heads, tq, dv), F32)],
        compiler_params=_cparams(2), name="diff_attn")(
            lam_vecs, subln_g.reshape(1, dv), main, main, main)


def _sort_key(x):
    bits = pltpu.bitcast(x + 0.0, I32)
    return jnp.where(bits < 0, bits ^ 0x7FFFFFFF, bits)


def _dsa_kernel(q_ref, k_ref, vt_ref, iq_ref, ik_ref, iwt_ref, o_ref, key_sc, mask_sc, qm_sc, acc_sc, *,
                tq, tk, seq, topk, slopes):
    qi = pl.program_id(1)
    q_pos0 = qi * tq
    nkb = (q_pos0 + tq + tk - 1) // tk
    n_tail = seq - nkb * tk
    t_row = q_pos0 + lax.broadcasted_iota(I32, (1, tq), 1)
    chunk_end = ((t_row >> CHUNK_SHIFT) + 1) << CHUNK_SHIFT
    s_col = lax.broadcasted_iota(I32, (tk, tq), 0)

    iq = iq_ref[0].astype(BF16)
    iq_heads = [iq[:, j * IDX_DIM:(j + 1) * IDX_DIM] for j in range(IDX_HEADS)]
    iw_rows = [iwt_ref[j:j + 1, :] for j in range(IDX_HEADS)]

    def score_block(kb, _):
        off = pl.multiple_of(kb * tk, tk)
        ik = ik_ref[0, pl.ds(off, tk), :].astype(BF16)
        acc = jnp.zeros((tk, tq), F32)
        for j in range(IDX_HEADS):
            acc = acc + iw_rows[j] * jnp.maximum(_dot_nt(ik, iq_heads[j]), 0.0)
        key_sc[kb] = _sort_key(jnp.where(off + s_col < chunk_end, acc, NEG_INF))
        return 0

    lax.fori_loop(0, nkb, score_block, 0)

    def count(pred):
        def body(kb, c):
            return c + jnp.sum(pred(key_sc[kb]).astype(I32), axis=0, keepdims=True)
        return lax.fori_loop(0, nkb, body, jnp.zeros((1, tq), I32))

    bits_per_check = 4

    def searching(st):
        i, _, done = st
        return jnp.logical_and(i < 32, jnp.min(done) == 0)

    def bit_steps(st):
        i, r, done = st
        for u in range(bits_per_check):
            cand = jnp.where(done > 0, r, r ^ (jnp.int32(1) << (31 - i - u)))
            cnt = count(lambda kk: kk >= cand) + jnp.where(cand <= KEY_NEG, n_tail, 0)
            r = jnp.where(cnt >= topk, cand, r)
            done = jnp.where(cnt == topk, 1, done)
        return i + bits_per_check, r, done

    _, tau, _ = lax.while_loop(
        searching, bit_steps, (jnp.int32(0), jnp.full((1, tq), -2 ** 31, I32), jnp.zeros((1, tq), I32)))
    cnt_gt = count(lambda kk: kk > tau) + jnp.where(tau < KEY_NEG, n_tail, 0)
    cnt_eq = count(lambda kk: kk == tau) + jnp.where(tau == KEY_NEG, n_tail, 0)
    need = topk - cnt_gt
    has_ties = jnp.max(cnt_eq - need) > 0

    @pl.when(jnp.logical_not(has_ties))
    def _():
        def body(kb, _):
            sel = jnp.logical_and(key_sc[kb] >= tau, kb * tk + s_col < chunk_end)
            mask_sc[kb] = jnp.where(sel, 0.0, NEG_INF)
            return 0
        lax.fori_loop(0, nkb, body, 0)

    @pl.when(has_ties)
    def _():
        r_i = lax.broadcasted_iota(I32, (tk, tk), 0)
        c_i = lax.broadcasted_iota(I32, (tk, tk), 1)
        earlier = jnp.where(c_i < r_i, 1.0, 0.0).astype(BF16)

        def body(kb, seen):
            kk = key_sc[kb]
            eq = kk == tau
            eqf = jnp.where(eq, 1.0, 0.0)
            rank = seen + jnp.dot(earlier, eqf.astype(BF16), preferred_element_type=F32)
            take = jnp.logical_and(eq, rank < need.astype(F32))
            sel = jnp.logical_and(jnp.logical_or(kk > tau, take), kb * tk + s_col < chunk_end)
            mask_sc[kb] = jnp.where(sel, 0.0, NEG_INF)
            return seen + jnp.sum(eqf, axis=0, keepdims=True)
        lax.fori_loop(0, nkb, body, jnp.zeros((1, tq), F32))

    hdim = DSA_HEAD_DIM
    pair = 2 * hdim
    n_heads = len(slopes)
    lane = lax.broadcasted_iota(I32, (tq, pair), 1)
    for p in range(n_heads // 2):
        qs = q_ref[0, :, p * pair:(p + 1) * pair] * (hdim ** -0.5)
        zero = jnp.zeros_like(qs)
        qm_sc[2 * p] = jnp.where(lane < hdim, qs, zero)
        qm_sc[2 * p + 1] = jnp.where(lane >= hdim, qs, zero)
    acc_sc[...] = jnp.zeros_like(acc_sc)

    def attend(kb, carry):
        off = pl.multiple_of(kb * tk, tk)
        dist = jnp.abs(t_row - (off + s_col)).astype(F32)
        madd = mask_sc[kb]
        out = []
        for hd in range(n_heads):
            k = k_ref[0, pl.ds(off, tk), (hd // 2) * pair:(hd // 2 + 1) * pair]
            s = _dot_nt(k, qm_sc[hd]) - slopes[hd] * dist + madd
            m, l = carry[2 * hd], carry[2 * hd + 1]
            m_new = jnp.maximum(m, jnp.max(s, axis=0, keepdims=True))
            pr = jnp.exp(s - m_new)
            a = jnp.exp(m - m_new)
            rows = slice(hd * hdim, (hd + 1) * hdim)
            acc_sc[rows, :] = a * acc_sc[rows, :] + jnp.dot(vt_ref[kb, rows, :], pr.astype(BF16),
                                                            preferred_element_type=F32)
            out.extend((m_new, a * l + jnp.sum(pr, axis=0, keepdims=True)))
        return tuple(out)

    m0 = jnp.full((1, tq), -jnp.inf, F32)
    l0 = jnp.zeros((1, tq), F32)
    carry = lax.fori_loop(0, nkb, attend, (m0, l0) * n_heads)
    for hd in range(n_heads):
        rows = slice(hd * hdim, (hd + 1) * hdim)
        acc_sc[rows, :] = acc_sc[rows, :] / carry[2 * hd + 1]
    o_ref[0] = acc_sc[...].T.astype(o_ref.dtype)


def _dsa_attention(main, vt, iq, ik, iwt, n_heads, col_q, col_k, vrow0, tq=256):
    B, S, _ = main.shape
    tk = vt.shape[-1]
    width = n_heads * DSA_HEAD_DIM
    nq = S // tq
    topk = min(IDX_TOPK_MAX, S // 4)
    slopes = tuple(2.0 ** (-8.0 * (i + 1) / n_heads) for i in range(n_heads))
    qrow = lambda n, c: pl.BlockSpec((1, tq, n), lambda b, q: (b, q, c))
    seqb = lambda n, c: pl.BlockSpec((1, S, n), lambda b, q: (b, 0, c))
    return pl.pallas_call(
        functools.partial(_dsa_kernel, tq=tq, tk=tk, seq=S, topk=topk, slopes=slopes),
        grid=(B, nq),
        in_specs=[qrow(width, col_q), seqb(width, col_k),
                  pl.BlockSpec((S // tk, width, tk), lambda b, q: (b, vrow0, 0)),
                  qrow(iq.shape[-1], 0), seqb(ik.shape[-1], 0),
                  pl.BlockSpec((iwt.shape[0], tq), lambda b, q: (0, b * nq + q))],
        out_specs=qrow(width, 0),
        out_shape=jax.ShapeDtypeStruct((B, S, width), BF16),
        scratch_shapes=[pltpu.VMEM((S // tk, tk, tq), I32), pltpu.VMEM((S // tk, tk, tq), F32),
                        pltpu.VMEM((n_heads, tq, 2 * DSA_HEAD_DIM), BF16),
                        pltpu.VMEM((width, tq), F32)],
        compiler_params=_cparams(2), name="dsa_attn")(main, main, vt, iq, ik, iwt)


def _outproj_kernel(a_ref, b_ref, h_ref, w_ref, g_ref, beta_ref, o_ref, *, alpha):
    wa = a_ref.shape[1]
    y = jnp.dot(a_ref[...], w_ref[:wa, :], preferred_element_type=F32)
    y = y + jnp.dot(b_ref[...], w_ref[wa:, :], preferred_element_type=F32)
    o_ref[...] = _layer_norm(alpha * h_ref[...] + y, g_ref[...], beta_ref[...])


def _output_projection(a, b, h, w_out, g, beta, alpha, tm=512):
    T, D = h.shape
    rows = lambda n: pl.BlockSpec((tm, n), lambda t: (t, 0))
    vec = pl.BlockSpec((1, D), lambda t: (0, 0))
    return pl.pallas_call(
        functools.partial(_outproj_kernel, alpha=alpha), grid=(T // tm,),
        in_specs=[rows(a.shape[1]), rows(b.shape[1]), rows(D),
                  pl.BlockSpec(w_out.shape, lambda t: (0, 0)), vec, vec],
        out_specs=rows(D), out_shape=jax.ShapeDtypeStruct((T, D), F32),
        compiler_params=_cparams(1), name="out_proj_ln1")(
            a, b, h, w_out, g.reshape(1, D), beta.reshape(1, D))


def _top_rows(x, n):
    rows = []
    rank = jnp.full(x.shape, float(n), F32)
    for b in range(n):
        mx = jnp.max(x, axis=0, keepdims=True)
        rows.append(mx)
        hit = x == mx
        rank = jnp.where(hit, float(b), rank)
        x = jnp.where(hit, -jnp.inf, x)
    return rows, rank


def _route_kernel(h_ref, wq_ref, sk_ref, n1_ref, c1_ref, r2_ref, e2_ref):
    hb = h_ref[...].astype(BF16)
    qd = 2 * PEER_HALF_DIM
    k = PEER_TOPK
    for hd in range(PEER_HEADS):
        q = jnp.dot(hb, wq_ref[:, hd * qd:(hd + 1) * qd], preferred_element_type=F32).astype(BF16)
        s1 = _dot_nt(sk_ref[hd, 0], q[:, :PEER_HALF_DIM])
        s2 = _dot_nt(sk_ref[hd, 1], q[:, PEER_HALF_DIM:])
        top1, _ = _top_rows(s1, k)
        top2, rank2 = _top_rows(s2, k)
        top2 = jnp.concatenate(top2, axis=0)
        half = k // 2
        cand = [top1[0] + top2] + [top1[a] + top2[:half] for a in range(1, half)]
        cand.append(jnp.concatenate(top1[half:], axis=0) + top2[0:1])
        best, _ = _top_rows(jnp.concatenate(cand, axis=0), k + 1)
        z = jnp.ones_like(best[0])
        for r in best[1:k]:
            z = z + jnp.exp(r - best[0])
        cut = 0.5 * (best[k - 1] + best[k])
        n1 = jnp.zeros_like(s1)
        for b in range(k):
            n1 = n1 + jnp.where(s1 >= cut - top2[b:b + 1], 1.0, 0.0)
        n1_ref[hd] = n1
        c1_ref[hd] = jnp.where(s1 >= top1[-1], jnp.exp(s1 - top1[0]) / z, 0.0)
        r2_ref[hd] = rank2.astype(BF16)
        e2_ref[hd] = jnp.where(s2 >= top2[k - 1:], jnp.exp(s2 - top2[0:1]), 0.0).astype(BF16)


def _peer_route(h, wq, subkeys, tm=256):
    T, D = h.shape
    shape = (PEER_HEADS, PEER_N_KEYS, T)
    ospec = pl.BlockSpec((PEER_HEADS, PEER_N_KEYS, tm), lambda t: (0, 0, t))
    return pl.pallas_call(
        _route_kernel, grid=(T // tm,),
        in_specs=[pl.BlockSpec((tm, D), lambda t: (t, 0)),
                  pl.BlockSpec(wq.shape, lambda t: (0, 0)),
                  pl.BlockSpec(subkeys.shape, lambda t: (0, 0, 0, 0))],
        out_specs=[ospec] * 4,
        out_shape=[jax.ShapeDtypeStruct(shape, F32), jax.ShapeDtypeStruct(shape, F32),
                   jax.ShapeDtypeStruct(shape, BF16), jax.ShapeDtypeStruct(shape, BF16)],
        compiler_params=_cparams(1), name="peer_route")(h, wq, subkeys)


def _gelu(x):
    return 0.5 * x * (1.0 + lax.erf(x * (2.0 ** -0.5)))


MXU_DIM = 256
GATE_ROWS = 16


def _peer_kernel(h_ref, u_ref, vt_ref, n1_ref, c1_ref, r2_ref, e2_ref, o_ref,
                 xb_sc, acc_sc, part_sc, act_a, act_b, w_a, w_b, *, te, n_tiles):
    s = pl.program_id(1)
    tm, d = h_ref.shape

    @pl.when(s == 0)
    def _():
        xb_sc[...] = h_ref[...].astype(BF16)
        acc_sc[...] = jnp.zeros_like(acc_sc)
        act_b[...] = jnp.zeros_like(act_b)
        w_a[...] = jnp.zeros_like(w_a)

    per = te // PEER_N_KEYS
    n_k = d // MXU_DIM

    def step(act_new, act_old, w_new, w_old):
        tile = jnp.clip(s - 1, 0, n_tiles - 1)

        def gate_piece(ii, lt):
            i = tile * per + ii
            lanes = slice(lt * LANES, (lt + 1) * LANES)
            nb = [jnp.broadcast_to(n1_ref[hd, pl.ds(i, 1), :][:, lanes].astype(BF16), (GATE_ROWS, LANES))
                  for hd in range(PEER_HEADS)]
            cb = [jnp.broadcast_to(c1_ref[hd, pl.ds(i, 1), :][:, lanes].astype(BF16), (GATE_ROWS, LANES))
                  for hd in range(PEER_HEADS)]
            zero = jnp.zeros((GATE_ROWS, LANES), BF16)
            for r0 in range(0, PEER_N_KEYS, GATE_ROWS):
                rows = slice(r0, r0 + GATE_ROWS)
                g = zero
                for hd in range(PEER_HEADS):
                    g = g + jnp.where(r2_ref[hd, rows, lanes] < nb[hd], e2_ref[hd, rows, lanes] * cb[hd], zero)
                arow = slice(ii * PEER_N_KEYS + r0, ii * PEER_N_KEYS + r0 + GATE_ROWS)
                w_new[arow, lanes] = g * act_old[arow, lanes]

        def stage1_piece(nc, kc):
            cols = slice(nc * MXU_DIM, (nc + 1) * MXU_DIM)
            kk = slice(kc * MXU_DIM, (kc + 1) * MXU_DIM)
            p = _dot_nt(u_ref[:, kk], xb_sc[cols, kk])
            if kc == 0:
                part_sc[:, cols] = p
            elif kc < n_k - 1:
                part_sc[:, cols] += p
            else:
                act_new[:, cols] = _gelu(part_sc[:, cols] + p).astype(BF16)

        def stage3_piece(nc, ec):
            cols = slice(nc * MXU_DIM, (nc + 1) * MXU_DIM)
            ee = slice(ec * MXU_DIM, (ec + 1) * MXU_DIM)
            acc_sc[:, cols] += jnp.dot(vt_ref[:, ee], w_old[ee, cols], preferred_element_type=F32)

        gates = [functools.partial(gate_piece, ii, lt) for ii in range(per) for lt in range(tm // LANES)]
        mxu = [functools.partial(stage1_piece, nc, kc) for nc in range(tm // MXU_DIM) for kc in range(n_k)]
        mxu += [functools.partial(stage3_piece, nc, ec) for nc in range(tm // MXU_DIM) for ec in range(te // MXU_DIM)]
        for b in range(max(len(gates), len(mxu))):
            if b < len(mxu):
                mxu[b]()
            if b < len(gates):
                gates[b]()

    @pl.when(s % 2 == 0)
    def _():
        step(act_a, act_b, w_b, w_a)

    @pl.when(s % 2 == 1)
    def _():
        step(act_b, act_a, w_a, w_b)

    @pl.when(s == n_tiles + 1)
    def _():
        o_ref[...] = acc_sc[...].T


def _peer_experts(h, u_bf, vt_bf, n1, c1, r2, e2, tm=512, te=512):
    T, D = h.shape
    E = u_bf.shape[0]
    n_tiles = E // te
    rspec = pl.BlockSpec((PEER_HEADS, PEER_N_KEYS, tm), lambda t, s: (0, 0, t))
    return pl.pallas_call(
        functools.partial(_peer_kernel, te=te, n_tiles=n_tiles), grid=(T // tm, n_tiles + 2),
        in_specs=[pl.BlockSpec((tm, D), lambda t, s: (t, 0)),
                  pl.BlockSpec((te, D), lambda t, s: (jnp.minimum(s, n_tiles - 1), 0)),
                  pl.BlockSpec((D, te), lambda t, s: (0, jnp.clip(s - 2, 0, n_tiles - 1))),
                  rspec, rspec, rspec, rspec],
        out_specs=pl.BlockSpec((tm, D), lambda t, s: (t, 0)),
        out_shape=jax.ShapeDtypeStruct((T, D), F32),
        scratch_shapes=[pltpu.VMEM((tm, D), BF16), pltpu.VMEM((D, tm), F32), pltpu.VMEM((te, tm), F32),
                        pltpu.VMEM((te, tm), BF16), pltpu.VMEM((te, tm), BF16),
                        pltpu.VMEM((te, tm), BF16), pltpu.VMEM((te, tm), BF16)],
        compiler_params=_cparams(2), name="peer_experts")(h, u_bf, vt_bf, n1, c1, r2, e2)


def _ple_kernel(h_ref, f_ref, p_ref, wg_ref, bg_ref, wp_ref, g_ref, beta_ref, o_ref, *, alpha):
    r = alpha * h_ref[...] + f_ref[...]
    z = jnp.dot(r.astype(BF16), wg_ref[...], preferred_element_type=F32) + bg_ref[...]
    gate = 1.0 / (1.0 + jnp.exp(-z))
    r = r + gate * jnp.dot(p_ref[...].astype(BF16), wp_ref[...], preferred_element_type=F32)
    o_ref[...] = _layer_norm(r, g_ref[...], beta_ref[...])


def _ple_ln2(h, f, p, layer, wg, bg, wp, g, beta, alpha, tm=512):
    T, D = h.shape
    pd = p.shape[-1]
    rows = pl.BlockSpec((tm, D), lambda t: (t, 0))
    vec = pl.BlockSpec((1, D), lambda t: (0, 0))
    return pl.pallas_call(
        functools.partial(_ple_kernel, alpha=alpha), grid=(T // tm,),
        in_specs=[rows, rows, pl.BlockSpec((None, tm, pd), lambda t: (layer, t, 0)),
                  pl.BlockSpec(wg.shape, lambda t: (0, 0)), vec,
                  pl.BlockSpec(wp.shape, lambda t: (0, 0)), vec, vec],
        out_specs=rows, out_shape=jax.ShapeDtypeStruct((T, D), F32),
        compiler_params=_cparams(1), name="ple_ln2")(
            h, f, p, wg, bg.reshape(1, D), wp, g.reshape(1, D), beta.reshape(1, D))


def _mixer_layer(h, B, S, w_in, w_out, lam_vecs, subln_g, lam_init, ln_g, ln_b, alpha):
    T, D = h.shape
    dw = D // 2
    n_diff = dw // (2 * DIFF_HEAD_DIM)
    sw = D - dw
    n_dsa = sw // DSA_HEAD_DIM
    n_iq = IDX_HEADS * IDX_DIM
    wb = w_in.astype(BF16)
    c_sv = 3 * dw + 2 * sw
    c_iq = c_sv + sw
    w_main = wb[:, :c_sv]
    w_vt = wb[:, c_sv:c_iq].T
    w_iq = wb[:, c_iq:c_iq + n_iq]
    w_ik = wb[:, c_iq + n_iq:c_iq + n_iq + IDX_DIM]
    w_iwt = jnp.pad(wb[:, c_iq + n_iq + IDX_DIM:].T, ((0, 8 - IDX_HEADS), (0, 0)))
    main, vt, iq, ik, iwt = _input_projection(h, w_main, w_vt, w_iq, w_ik, w_iwt)
    main = main.reshape(B, S, c_sv)
    a = _diff_attention(main, lam_vecs, subln_g, lam_init, n_diff, 0)
    b = _dsa_attention(main, vt, iq.reshape(B, S, -1), ik.reshape(B, S, -1), iwt, n_dsa,
                       3 * dw // sw, 3 * dw // sw + 1, 0)
    return _output_projection(a.reshape(T, -1), b.reshape(T, -1), h, w_out.astype(BF16), ln_g, ln_b, alpha)


def kernel(x, p, ln_in_g, ln_in_b, w_in, w_out, diff_lambda, diff_subln_g, ln1_g, ln1_b, peer_wq, peer_subkeys, peer_u, peer_v, ple_wg, ple_bg, ple_wp, ln2_g, ln2_b):
    B, S, D = x.shape
    depth = w_in.shape[0]
    T = B * S
    alpha = (2 * depth) ** 0.25
    h = _entry_layer_norm(x.reshape(T, D), ln_in_g, ln_in_b)
    p2 = p.reshape(depth, T, p.shape[-1])
    for i in range(depth):
        lam_init = 0.8 - 0.6 * math.exp(-0.3 * i)
        h = _mixer_layer(h, B, S, w_in[i], w_out[i], diff_lambda[i], diff_subln_g[i], lam_init,
                         ln1_g[i], ln1_b[i], alpha)
        n1, c1, r2, e2 = _peer_route(h, peer_wq[i].astype(BF16), peer_subkeys[i].astype(BF16))
        f = _peer_experts(h, peer_u[i].astype(BF16), peer_v[i].astype(BF16).T, n1, c1, r2, e2)
        h = _ple_ln2(h, f, p2, i, ple_wg[i].astype(BF16), ple_bg[i], ple_wp[i].astype(BF16),
                     ln2_g[i], ln2_b[i], alpha)
    return h.reshape(B, S, D)
```

```python
import functools
import math
import struct

import jax
import jax.numpy as jnp
from jax import lax
from jax.experimental import pallas as pl
from jax.experimental.pallas import tpu as pltpu

F32 = jnp.float32
BF16 = jnp.bfloat16
I32 = jnp.int32

LN_EPS = 1e-5
NEG_INF = -1e30
CHUNK = 64
CHUNK_SHIFT = 6
LANES = 128
VMEM_LIMIT = 56 * 1024 * 1024

DIFF_HEAD_DIM = 64
DSA_HEAD_DIM = 64
IDX_HEADS = 4
IDX_DIM = 64
IDX_TOPK_MAX = 256
PEER_HEADS = 8
PEER_N_KEYS = 128
PEER_HALF_DIM = 128
PEER_TOPK = 16

_NEG_BITS = struct.unpack("<i", struct.pack("<f", NEG_INF))[0]
KEY_NEG = _NEG_BITS ^ 0x7FFFFFFF


def _cparams(n_axes):
    return pltpu.CompilerParams(
        dimension_semantics=("arbitrary",) * n_axes, vmem_limit_bytes=VMEM_LIMIT)


def _dot_nt(a, b):
    return lax.dot_general(a, b, (((1,), (1,)), ((), ())), preferred_element_type=F32)


def _layer_norm(x, g, b):
    mu = jnp.mean(x, axis=-1, keepdims=True)
    xc = x - mu
    var = jnp.mean(xc * xc, axis=-1, keepdims=True)
    return xc * lax.rsqrt(var + LN_EPS) * g + b


def _ln_kernel(x_ref, g_ref, b_ref, o_ref):
    o_ref[...] = _layer_norm(x_ref[...], g_ref[...], b_ref[...])


def _entry_layer_norm(x, g, b, tm=512):
    T, D = x.shape
    row = pl.BlockSpec((tm, D), lambda t: (t, 0))
    vec = pl.BlockSpec((1, D), lambda t: (0, 0))
    return pl.pallas_call(
        _ln_kernel, grid=(T // tm,), in_specs=[row, vec, vec], out_specs=row,
        out_shape=jax.ShapeDtypeStruct((T, D), F32), compiler_params=_cparams(1),
        name="entry_ln")(x, g.reshape(1, D), b.reshape(1, D))


def _inproj_kernel(h_ref, wm_ref, wvt_ref, wiq_ref, wik_ref, wiwt_ref, main_ref, vt_ref, iq_ref, ik_ref, iwt_ref, *, tn):
    hb = h_ref[...].astype(BF16)
    for j in range(0, wm_ref.shape[1], tn):
        main_ref[:, j:j + tn] = jnp.dot(hb, wm_ref[:, j:j + tn], preferred_element_type=F32).astype(BF16)
    vt_ref[0] = _dot_nt(wvt_ref[...], hb).astype(BF16)
    iq_ref[...] = jnp.dot(hb, wiq_ref[...], preferred_element_type=F32) * (IDX_DIM ** -0.5)
    ik_ref[...] = jnp.dot(hb, wik_ref[...], preferred_element_type=F32)
    iwt_ref[...] = _dot_nt(wiwt_ref[...], hb) * (IDX_HEADS ** -0.5)


def _input_projection(h, w_main, w_vt, w_iq, w_ik, w_iwt, tm=512, tn=512):
    T, D = h.shape
    nm = w_main.shape[1]
    nv = w_vt.shape[0]

    def full(w):
        return pl.BlockSpec(w.shape, lambda t: (0, 0))

    def rows(n):
        return pl.BlockSpec((tm, n), lambda t: (t, 0))

    return pl.pallas_call(
        functools.partial(_inproj_kernel, tn=tn), grid=(T // tm,),
        in_specs=[rows(D), full(w_main), full(w_vt), full(w_iq), full(w_ik), full(w_iwt)],
        out_specs=[rows(nm), pl.BlockSpec((1, nv, tm), lambda t: (t, 0, 0)),
                   rows(w_iq.shape[1]), rows(w_ik.shape[1]),
                   pl.BlockSpec((w_iwt.shape[0], tm), lambda t: (0, t))],
        out_shape=[jax.ShapeDtypeStruct((T, nm), BF16),
                   jax.ShapeDtypeStruct((T // tm, nv, tm), BF16),
                   jax.ShapeDtypeStruct((T, w_iq.shape[1]), F32),
                   jax.ShapeDtypeStruct((T, w_ik.shape[1]), F32),
                   jax.ShapeDtypeStruct((w_iwt.shape[0], T), F32)],
        compiler_params=_cparams(1), name="in_proj")(h, w_main, w_vt, w_iq, w_ik, w_iwt)


def _diff_kernel(lamv_ref, g_ref, q_ref, k_ref, v_ref, o_ref, qm_sc, acc_sc, *, tq, tk, lam_init, slopes):
    qi = pl.program_id(1)
    dv = 2 * DIFF_HEAD_DIM
    n_heads = len(slopes)
    lane = lax.broadcasted_iota(I32, (tq, dv), 1)
    for hd in range(n_heads):
        qs = q_ref[0, :, hd * dv:(hd + 1) * dv] * (DIFF_HEAD_DIM ** -0.5)
        zero = jnp.zeros_like(qs)
        qm_sc[2 * hd] = jnp.where(lane < DIFF_HEAD_DIM, qs, zero)
        qm_sc[2 * hd + 1] = jnp.where(lane >= DIFF_HEAD_DIM, qs, zero)
    acc_sc[...] = jnp.zeros_like(acc_sc)
    t_idx = qi * tq + lax.broadcasted_iota(I32, (tq, tk), 0)

    def step(kb, carry, diagonal):
        off = pl.multiple_of(kb * tk, tk)
        s_idx = off + lax.broadcasted_iota(I32, (tq, tk), 1)
        dist = jnp.abs(t_idx - s_idx).astype(F32)
        if diagonal:
            visible = s_idx < (((t_idx >> CHUNK_SHIFT) + 1) << CHUNK_SHIFT)

        def qk(c):
            return _dot_nt(qm_sc[c], k_ref[0, pl.ds(off, tk), (c // 2) * dv:(c // 2 + 1) * dv])

        def pv(c, a, p):
            v = v_ref[0, pl.ds(off, tk), (c // 2) * dv:(c // 2 + 1) * dv]
            acc_sc[c] = a * acc_sc[c] + jnp.dot(p, v, preferred_element_type=F32)

        out = []
        s_next = qk(0)
        pending = None
        for c in range(2 * n_heads):
            s = s_next - slopes[c // 2] * dist
            if c + 1 < 2 * n_heads:
                s_next = qk(c + 1)
            if diagonal:
                s = jnp.where(visible, s, NEG_INF)
            m, l = carry[2 * c], carry[2 * c + 1]
            m_new = jnp.maximum(m, jnp.max(s, axis=-1, keepdims=True))
            p = jnp.exp(s - m_new)
            a = jnp.exp(m - m_new)
            out.extend((m_new, a * l + jnp.sum(p, axis=-1, keepdims=True)))
            if pending is not None:
                pv(*pending)
            pending = (c, a, p.astype(BF16))
        pv(*pending)
        return tuple(out)

    m0 = jnp.full((tq, 1), -jnp.inf, F32)
    l0 = jnp.zeros((tq, 1), F32)
    n_full = (qi * tq) // tk
    carry = lax.fori_loop(0, n_full, functools.partial(step, diagonal=False), (m0, l0) * (2 * n_heads))
    for d in range(max(tq // tk, 1)):
        carry = step(n_full + d, carry, True)

    lv = lamv_ref[...]
    lam = (jnp.exp(jnp.sum(lv[0:1] * lv[1:2], axis=-1, keepdims=True))
           - jnp.exp(jnp.sum(lv[2:3] * lv[3:4], axis=-1, keepdims=True)) + lam_init)
    for hd in range(n_heads):
        o = acc_sc[2 * hd] / carry[4 * hd + 1] - lam * (acc_sc[2 * hd + 1] / carry[4 * hd + 3])
        o = o * lax.rsqrt(jnp.mean(o * o, axis=-1, keepdims=True) + LN_EPS) * g_ref[...]
        o_ref[0, :, hd * dv:(hd + 1) * dv] = (o * (1.0 - lam_init)).astype(o_ref.dtype)


def _diff_attention(main, lam_vecs, subln_g, lam_init, n_heads, col0, tq=256, tk=512):
    B, S, _ = main.shape
    dv = 2 * DIFF_HEAD_DIM
    width = n_heads * dv
    slopes = tuple(2.0 ** (-8.0 * (i + 1) / n_heads) for i in range(n_heads))
    kv = lambda c: pl.BlockSpec((1, S, width), lambda b, q: (b, 0, c))
    qo = lambda c: pl.BlockSpec((1, tq, width), lambda b, q: (b, q, c))
    return pl.pallas_call(
        functools.partial(_diff_kernel, tq=tq, tk=tk, lam_init=lam_init, slopes=slopes),
        grid=(B, S // tq),
        in_specs=[pl.BlockSpec(lam_vecs.shape, lambda b, q: (0, 0)),
                  pl.BlockSpec((1, dv), lambda b, q: (0, 0)),
                  qo(col0), kv(col0 + 1), kv(col0 + 2)],
        out_specs=qo(0),
        out_shape=jax.ShapeDtypeStruct((B, S, width), BF16),
        scratch_shapes=[pltpu.VMEM((2 * n_heads, tq, dv), BF16), pltpu.VMEM((2 * n_heads, tq, dv), F32)],
        compiler_params=_cparams(2), name="diff_attn")(
            lam_vecs, subln_g.reshape(1, dv), main, main, main)


def _sort_key(x):
    bits = pltpu.bitcast(x + 0.0, I32)
    return jnp.where(bits < 0, bits ^ 0x7FFFFFFF, bits)


def _dsa_kernel(q_ref, k_ref, vt_ref, iq_ref, ik_ref, iwt_ref, o_ref, key_sc, mask_sc, qm_sc, acc_sc, *,
                tq, tk, seq, topk, slopes):
    qi = pl.program_id(1)
    q_pos0 = qi * tq
    nkb = (q_pos0 + tq + tk - 1) // tk
    n_tail = seq - nkb * tk
    t_row = q_pos0 + lax.broadcasted_iota(I32, (1, tq), 1)
    chunk_end = ((t_row >> CHUNK_SHIFT) + 1) << CHUNK_SHIFT
    s_col = lax.broadcasted_iota(I32, (tk, tq), 0)

    iq = iq_ref[0].astype(BF16)
    iq_heads = [iq[:, j * IDX_DIM:(j + 1) * IDX_DIM] for j in range(IDX_HEADS)]
    iw_rows = [iwt_ref[j:j + 1, :] for j in range(IDX_HEADS)]

    def score_block(kb, _):
        off = pl.multiple_of(kb * tk, tk)
        ik = ik_ref[0, pl.ds(off, tk), :].astype(BF16)
        acc = jnp.zeros((tk, tq), F32)
        for j in range(IDX_HEADS):
            acc = acc + iw_rows[j] * jnp.maximum(_dot_nt(ik, iq_heads[j]), 0.0)
        key_sc[kb] = _sort_key(jnp.where(off + s_col < chunk_end, acc, NEG_INF))
        return 0

    lax.fori_loop(0, nkb, score_block, 0)

    def count(pred):
        def body(kb, c):
            return c + jnp.sum(pred(key_sc[kb]).astype(I32), axis=0, keepdims=True)
        return lax.fori_loop(0, nkb, body, jnp.zeros((1, tq), I32))

    bits_per_check = 4

    def searching(st):
        i, _, done = st
        return jnp.logical_and(i < 32, jnp.min(done) == 0)

    def bit_steps(st):
        i, r, done = st
        for u in range(bits_per_check):
            cand = jnp.where(done > 0, r, r ^ (jnp.int32(1) << (31 - i - u)))
            cnt = count(lambda kk: kk >= cand) + jnp.where(cand <= KEY_NEG, n_tail, 0)
            r = jnp.where(cnt >= topk, cand, r)
            done = jnp.where(cnt == topk, 1, done)
        return i + bits_per_check, r, done

    _, tau, _ = lax.while_loop(
        searching, bit_steps, (jnp.int32(0), jnp.full((1, tq), -2 ** 31, I32), jnp.zeros((1, tq), I32)))
    cnt_gt = count(lambda kk: kk > tau) + jnp.where(tau < KEY_NEG, n_tail, 0)
    cnt_eq = count(lambda kk: kk == tau) + jnp.where(tau == KEY_NEG, n_tail, 0)
    need = topk - cnt_gt
    has_ties = jnp.max(cnt_eq - need) > 0

    @pl.when(jnp.logical_not(has_ties))
    def _():
        def body(kb, _):
            sel = jnp.logical_and(key_sc[kb] >= tau, kb * tk + s_col < chunk_end)
            mask_sc[kb] = jnp.where(sel, 0.0, NEG_INF)
            return 0
        lax.fori_loop(0, nkb, body, 0)

    @pl.when(has_ties)
    def _():
        r_i = lax.broadcasted_iota(I32, (tk, tk), 0)
        c_i = lax.broadcasted_iota(I32, (tk, tk), 1)
        earlier = jnp.where(c_i < r_i, 1.0, 0.0).astype(BF16)

        def body(kb, seen):
            kk = key_sc[kb]
            eq = kk == tau
            eqf = jnp.where(eq, 1.0, 0.0)
            rank = seen + jnp.dot(earlier, eqf.astype(BF16), preferred_element_type=F32)
            take = jnp.logical_and(eq, rank < need.astype(F32))
            sel = jnp.logical_and(jnp.logical_or(kk > tau, take), kb * tk + s_col < chunk_end)
            mask_sc[kb] = jnp.where(sel, 0.0, NEG_INF)
            return seen + jnp.sum(eqf, axis=0, keepdims=True)
        lax.fori_loop(0, nkb, body, jnp.zeros((1, tq), F32))

    hdim = DSA_HEAD_DIM
    pair = 2 * hdim
    n_heads = len(slopes)
    lane = lax.broadcasted_iota(I32, (tq, pair), 1)
    for p in range(n_heads // 2):
        qs = q_ref[0, :, p * pair:(p + 1) * pair] * (hdim ** -0.5)
        zero = jnp.zeros_like(qs)
        qm_sc[2 * p] = jnp.where(lane < hdim, qs, zero)
        qm_sc[2 * p + 1] = jnp.where(lane >= hdim, qs, zero)
    acc_sc[...] = jnp.zeros_like(acc_sc)

    def attend(kb, carry):
        off = pl.multiple_of(kb * tk, tk)
        dist = jnp.abs(t_row - (off + s_col)).astype(F32)
        madd = mask_sc[kb]

        def qk(hd):
            k = k_ref[0, pl.ds(off, tk), (hd // 2) * pair:(hd // 2 + 1) * pair]
            return _dot_nt(k, qm_sc[hd])

        def pv(hd, a, pr):
            rows = slice(hd * hdim, (hd + 1) * hdim)
            acc_sc[rows, :] = a * acc_sc[rows, :] + jnp.dot(vt_ref[kb, rows, :], pr, preferred_element_type=F32)

        out = []
        s_next = qk(0)
        pending = None
        for hd in range(n_heads):
            s = s_next - slopes[hd] * dist + madd
            if hd + 1 < n_heads:
                s_next = qk(hd + 1)
            m, l = carry[2 * hd], carry[2 * hd + 1]
            m_new = jnp.maximum(m, jnp.max(s, axis=0, keepdims=True))
            pr = jnp.exp(s - m_new)
            a = jnp.exp(m - m_new)
            out.extend((m_new, a * l + jnp.sum(pr, axis=0, keepdims=True)))
            if pending is not None:
                pv(*pending)
            pending = (hd, a, pr.astype(BF16))
        pv(*pending)
        return tuple(out)

    m0 = jnp.full((1, tq), -jnp.inf, F32)
    l0 = jnp.zeros((1, tq), F32)
    carry = lax.fori_loop(0, nkb, attend, (m0, l0) * n_heads)
    for hd in range(n_heads):
        rows = slice(hd * hdim, (hd + 1) * hdim)
        acc_sc[rows, :] = acc_sc[rows, :] / carry[2 * hd + 1]
    o_ref[0] = acc_sc[...].T.astype(o_ref.dtype)


def _dsa_attention(main, vt, iq, ik, iwt, n_heads, col_q, col_k, vrow0, tq=256):
    B, S, _ = main.shape
    tk = vt.shape[-1]
    width = n_heads * DSA_HEAD_DIM
    nq = S // tq
    topk = min(IDX_TOPK_MAX, S // 4)
    slopes = tuple(2.0 ** (-8.0 * (i + 1) / n_heads) for i in range(n_heads))
    qrow = lambda n, c: pl.BlockSpec((1, tq, n), lambda b, q: (b, q, c))
    seqb = lambda n, c: pl.BlockSpec((1, S, n), lambda b, q: (b, 0, c))
    return pl.pallas_call(
        functools.partial(_dsa_kernel, tq=tq, tk=tk, seq=S, topk=topk, slopes=slopes),
        grid=(B, nq),
        in_specs=[qrow(width, col_q), seqb(width, col_k),
                  pl.BlockSpec((S // tk, width, tk), lambda b, q: (b, vrow0, 0)),
                  qrow(iq.shape[-1], 0), seqb(ik.shape[-1], 0),
                  pl.BlockSpec((iwt.shape[0], tq), lambda b, q: (0, b * nq + q))],
        out_specs=qrow(width, 0),
        out_shape=jax.ShapeDtypeStruct((B, S, width), BF16),
        scratch_shapes=[pltpu.VMEM((S // tk, tk, tq), I32), pltpu.VMEM((S // tk, tk, tq), F32),
                        pltpu.VMEM((n_heads, tq, 2 * DSA_HEAD_DIM), BF16),
                        pltpu.VMEM((width, tq), F32)],
        compiler_params=_cparams(2), name="dsa_attn")(main, main, vt, iq, ik, iwt)


def _outproj_kernel(a_ref, b_ref, h_ref, w_ref, g_ref, beta_ref, o_ref, *, alpha):
    wa = a_ref.shape[1]
    y = jnp.dot(a_ref[...], w_ref[:wa, :], preferred_element_type=F32)
    y = y + jnp.dot(b_ref[...], w_ref[wa:, :], preferred_element_type=F32)
    o_ref[...] = _layer_norm(alpha * h_ref[...] + y, g_ref[...], beta_ref[...])


def _output_projection(a, b, h, w_out, g, beta, alpha, tm=512):
    T, D = h.shape
    rows = lambda n: pl.BlockSpec((tm, n), lambda t: (t, 0))
    vec = pl.BlockSpec((1, D), lambda t: (0, 0))
    return pl.pallas_call(
        functools.partial(_outproj_kernel, alpha=alpha), grid=(T // tm,),
        in_specs=[rows(a.shape[1]), rows(b.shape[1]), rows(D),
                  pl.BlockSpec(w_out.shape, lambda t: (0, 0)), vec, vec],
        out_specs=rows(D), out_shape=jax.ShapeDtypeStruct((T, D), F32),
        compiler_params=_cparams(1), name="out_proj_ln1")(
            a, b, h, w_out, g.reshape(1, D), beta.reshape(1, D))


def _top_rows(x, n):
    rows = []
    rank = jnp.full(x.shape, float(n), F32)
    for b in range(n):
        mx = jnp.max(x, axis=0, keepdims=True)
        rows.append(mx)
        hit = x == mx
        rank = jnp.where(hit, float(b), rank)
        x = jnp.where(hit, -jnp.inf, x)
    return rows, rank


def _route_kernel(h_ref, wq_ref, sk_ref, n1_ref, c1_ref, r2_ref, e2_ref):
    hb = h_ref[...].astype(BF16)
    qd = 2 * PEER_HALF_DIM
    k = PEER_TOPK
    for hd in range(PEER_HEADS):
        q = jnp.dot(hb, wq_ref[:, hd * qd:(hd + 1) * qd], preferred_element_type=F32).astype(BF16)
        s1 = _dot_nt(sk_ref[hd, 0], q[:, :PEER_HALF_DIM])
        s2 = _dot_nt(sk_ref[hd, 1], q[:, PEER_HALF_DIM:])
        top1, _ = _top_rows(s1, k)
        top2, rank2 = _top_rows(s2, k)
        top2 = jnp.concatenate(top2, axis=0)
        half = k // 2
        cand = [top1[0] + top2] + [top1[a] + top2[:half] for a in range(1, half)]
        cand.append(jnp.concatenate(top1[half:], axis=0) + top2[0:1])
        best, _ = _top_rows(jnp.concatenate(cand, axis=0), k + 1)
        z = jnp.ones_like(best[0])
        for r in best[1:k]:
            z = z + jnp.exp(r - best[0])
        cut = 0.5 * (best[k - 1] + best[k])
        n1 = jnp.zeros_like(s1)
        for b in range(k):
            n1 = n1 + jnp.where(s1 >= cut - top2[b:b + 1], 1.0, 0.0)
        n1_ref[hd] = n1
        c1_ref[hd] = jnp.where(s1 >= top1[-1], jnp.exp(s1 - top1[0]) / z, 0.0)
        r2_ref[hd] = rank2.astype(BF16)
        e2_ref[hd] = jnp.where(s2 >= top2[k - 1:], jnp.exp(s2 - top2[0:1]), 0.0).astype(BF16)


def _peer_route(h, wq, subkeys, tm=256):
    T, D = h.shape
    shape = (PEER_HEADS, PEER_N_KEYS, T)
    ospec = pl.BlockSpec((PEER_HEADS, PEER_N_KEYS, tm), lambda t: (0, 0, t))
    return pl.pallas_call(
        _route_kernel, grid=(T // tm,),
        in_specs=[pl.BlockSpec((tm, D), lambda t: (t, 0)),
                  pl.BlockSpec(wq.shape, lambda t: (0, 0)),
                  pl.BlockSpec(subkeys.shape, lambda t: (0, 0, 0, 0))],
        out_specs=[ospec] * 4,
        out_shape=[jax.ShapeDtypeStruct(shape, F32), jax.ShapeDtypeStruct(shape, F32),
                   jax.ShapeDtypeStruct(shape, BF16), jax.ShapeDtypeStruct(shape, BF16)],
        compiler_params=_cparams(1), name="peer_route")(h, wq, subkeys)


def _gelu(x):
    return 0.5 * x * (1.0 + lax.erf(x * (2.0 ** -0.5)))


MXU_DIM = 256
GATE_ROWS = 16


def _peer_kernel(h_ref, u_ref, vt_ref, n1_ref, c1_ref, r2_ref, e2_ref, o_ref,
                 xb_sc, acc_sc, part_sc, act_a, act_b, w_a, w_b, *, te, n_tiles):
    s = pl.program_id(1)
    tm, d = h_ref.shape

    @pl.when(s == 0)
    def _():
        xb_sc[...] = h_ref[...].astype(BF16)
        acc_sc[...] = jnp.zeros_like(acc_sc)
        act_b[...] = jnp.zeros_like(act_b)
        w_a[...] = jnp.zeros_like(w_a)

    per = te // PEER_N_KEYS
    n_k = d // MXU_DIM

    def step(act_new, act_old, w_new, w_old):
        tile = jnp.clip(s - 1, 0, n_tiles - 1)

        def gate_piece(ii, lt):
            i = tile * per + ii
            lanes = slice(lt * LANES, (lt + 1) * LANES)
            nb = [jnp.broadcast_to(n1_ref[hd, pl.ds(i, 1), :][:, lanes].astype(BF16), (GATE_ROWS, LANES))
                  for hd in range(PEER_HEADS)]
            cb = [jnp.broadcast_to(c1_ref[hd, pl.ds(i, 1), :][:, lanes].astype(BF16), (GATE_ROWS, LANES))
                  for hd in range(PEER_HEADS)]
            zero = jnp.zeros((GATE_ROWS, LANES), BF16)
            for r0 in range(0, PEER_N_KEYS, GATE_ROWS):
                rows = slice(r0, r0 + GATE_ROWS)
                g = zero
                for hd in range(PEER_HEADS):
                    g = g + jnp.where(r2_ref[hd, rows, lanes] < nb[hd], e2_ref[hd, rows, lanes] * cb[hd], zero)
                arow = slice(ii * PEER_N_KEYS + r0, ii * PEER_N_KEYS + r0 + GATE_ROWS)
                w_new[arow, lanes] = g * act_old[arow, lanes]

        def stage1_piece(nc, kc):
            cols = slice(nc * MXU_DIM, (nc + 1) * MXU_DIM)
            kk = slice(kc * MXU_DIM, (kc + 1) * MXU_DIM)
            p = _dot_nt(u_ref[:, kk], xb_sc[cols, kk])
            if kc == 0:
                part_sc[:, cols] = p
            elif kc < n_k - 1:
                part_sc[:, cols] += p
            else:
                act_new[:, cols] = _gelu(part_sc[:, cols] + p).astype(BF16)

        def stage3_piece(nc, ec):
            cols = slice(nc * MXU_DIM, (nc + 1) * MXU_DIM)
            ee = slice(ec * MXU_DIM, (ec + 1) * MXU_DIM)
            acc_sc[:, cols] += jnp.dot(vt_ref[:, ee], w_old[ee, cols], preferred_element_type=F32)

        gates = [functools.partial(gate_piece, ii, lt) for ii in range(per) for lt in range(tm // LANES)]
        mxu = [functools.partial(stage1_piece, nc, kc) for nc in range(tm // MXU_DIM) for kc in range(n_k)]
        mxu += [functools.partial(stage3_piece, nc, ec) for nc in range(tm // MXU_DIM) for ec in range(te // MXU_DIM)]
        for b in range(max(len(gates), len(mxu))):
            if b < len(mxu):
                mxu[b]()
            if b < len(gates):
                gates[b]()

    @pl.when(s % 2 == 0)
    def _():
        step(act_a, act_b, w_b, w_a)

    @pl.when(s % 2 == 1)
    def _():
        step(act_b, act_a, w_a, w_b)

    @pl.when(s == n_tiles + 1)
    def _():
        o_ref[...] = acc_sc[...].T


def _peer_experts(h, u_bf, vt_bf, n1, c1, r2, e2, tm=512, te=512):
    T, D = h.shape
    E = u_bf.shape[0]
    n_tiles = E // te
    rspec = pl.BlockSpec((PEER_HEADS, PEER_N_KEYS, tm), lambda t, s: (0, 0, t))
    return pl.pallas_call(
        functools.partial(_peer_kernel, te=te, n_tiles=n_tiles), grid=(T // tm, n_tiles + 2),
        in_specs=[pl.BlockSpec((tm, D), lambda t, s: (t, 0)),
                  pl.BlockSpec((te, D), lambda t, s: (jnp.minimum(s, n_tiles - 1), 0)),
                  pl.BlockSpec((D, te), lambda t, s: (0, jnp.clip(s - 2, 0, n_tiles - 1))),
                  rspec, rspec, rspec, rspec],
        out_specs=pl.BlockSpec((tm, D), lambda t, s: (t, 0)),
        out_shape=jax.ShapeDtypeStruct((T, D), F32),
        scratch_shapes=[pltpu.VMEM((tm, D), BF16), pltpu.VMEM((D, tm), F32), pltpu.VMEM((te, tm), F32),
                        pltpu.VMEM((te, tm), BF16), pltpu.VMEM((te, tm), BF16),
                        pltpu.VMEM((te, tm), BF16), pltpu.VMEM((te, tm), BF16)],
        compiler_params=_cparams(2), name="peer_experts")(h, u_bf, vt_bf, n1, c1, r2, e2)


def _ple_kernel(h_ref, f_ref, p_ref, wg_ref, bg_ref, wp_ref, g_ref, beta_ref, o_ref, *, alpha):
    r = alpha * h_ref[...] + f_ref[...]
    z = jnp.dot(r.astype(BF16), wg_ref[...], preferred_element_type=F32) + bg_ref[...]
    gate = 1.0 / (1.0 + jnp.exp(-z))
    r = r + gate * jnp.dot(p_ref[...].astype(BF16), wp_ref[...], preferred_element_type=F32)
    o_ref[...] = _layer_norm(r, g_ref[...], beta_ref[...])


def _ple_ln2(h, f, p, layer, wg, bg, wp, g, beta, alpha, tm=512):
    T, D = h.shape
    pd = p.shape[-1]
    rows = pl.BlockSpec((tm, D), lambda t: (t, 0))
    vec = pl.BlockSpec((1, D), lambda t: (0, 0))
    return pl.pallas_call(
        functools.partial(_ple_kernel, alpha=alpha), grid=(T // tm,),
        in_specs=[rows, rows, pl.BlockSpec((None, tm, pd), lambda t: (layer, t, 0)),
                  pl.BlockSpec(wg.shape, lambda t: (0, 0)), vec,
                  pl.BlockSpec(wp.shape, lambda t: (0, 0)), vec, vec],
        out_specs=rows, out_shape=jax.ShapeDtypeStruct((T, D), F32),
        compiler_params=_cparams(1), name="ple_ln2")(
            h, f, p, wg, bg.reshape(1, D), wp, g.reshape(1, D), beta.reshape(1, D))


def _mixer_layer(h, B, S, w_in, w_out, lam_vecs, subln_g, lam_init, ln_g, ln_b, alpha):
    T, D = h.shape
    dw = D // 2
    n_diff = dw // (2 * DIFF_HEAD_DIM)
    sw = D - dw
    n_dsa = sw // DSA_HEAD_DIM
    n_iq = IDX_HEADS * IDX_DIM
    wb = w_in.astype(BF16)
    c_sv = 3 * dw + 2 * sw
    c_iq = c_sv + sw
    w_main = wb[:, :c_sv]
    w_vt = wb[:, c_sv:c_iq].T
    w_iq = wb[:, c_iq:c_iq + n_iq]
    w_ik = wb[:, c_iq + n_iq:c_iq + n_iq + IDX_DIM]
    w_iwt = jnp.pad(wb[:, c_iq + n_iq + IDX_DIM:].T, ((0, 8 - IDX_HEADS), (0, 0)))
    main, vt, iq, ik, iwt = _input_projection(h, w_main, w_vt, w_iq, w_ik, w_iwt)
    main = main.reshape(B, S, c_sv)
    a = _diff_attention(main, lam_vecs, subln_g, lam_init, n_diff, 0)
    b = _dsa_attention(main, vt, iq.reshape(B, S, -1), ik.reshape(B, S, -1), iwt, n_dsa,
                       3 * dw // sw, 3 * dw // sw + 1, 0)
    return _output_projection(a.reshape(T, -1), b.reshape(T, -1), h, w_out.astype(BF16), ln_g, ln_b, alpha)


def kernel(x, p, ln_in_g, ln_in_b, w_in, w_out, diff_lambda, diff_subln_g, ln1_g, ln1_b, peer_wq, peer_subkeys, peer_u, peer_v, ple_wg, ple_bg, ple_wp, ln2_g, ln2_b):
    B, S, D = x.shape
    depth = w_in.shape[0]
    T = B * S
    alpha = (2 * depth) ** 0.25
    h = _entry_layer_norm(x.reshape(T, D), ln_in_g, ln_in_b)
    p2 = p.reshape(depth, T, p.shape[-1])
    for i in range(depth):
        lam_init = 0.8 - 0.6 * math.exp(-0.3 * i)
        h = _mixer_layer(h, B, S, w_in[i], w_out[i], diff_lambda[i], diff_subln_g[i], lam_init,
                         ln1_g[i], ln1_b[i], alpha)
        n1, c1, r2, e2 = _peer_route(h, peer_wq[i].astype(BF16), peer_subkeys[i].astype(BF16))
        f = _peer_experts(h, peer_u[i].astype(BF16), peer_v[i].astype(BF16).T, n1, c1, r2, e2)
        h = _ple_ln2(h, f, p2, i, ple_wg[i].astype(BF16), ple_bg[i], ple_wp[i].astype(BF16),
                     ln2_g[i], ln2_b[i], alpha)
    return h.reshape(B, S, D)
```

```python
import functools
import math
import struct

import jax
import jax.numpy as jnp
from jax import lax
from jax.experimental import pallas as pl
from jax.experimental.pallas import tpu as pltpu

F32 = jnp.float32
BF16 = jnp.bfloat16
I32 = jnp.int32

LN_EPS = 1e-5
NEG_INF = -1e30
CHUNK = 64
CHUNK_SHIFT = 6
LANES = 128
VMEM_LIMIT = 56 * 1024 * 1024

DIFF_HEAD_DIM = 64
DSA_HEAD_DIM = 64
IDX_HEADS = 4
IDX_DIM = 64
IDX_TOPK_MAX = 256
PEER_HEADS = 8
PEER_N_KEYS = 128
PEER_HALF_DIM = 128
PEER_TOPK = 16

_NEG_BITS = struct.unpack("<i", struct.pack("<f", NEG_INF))[0]
KEY_NEG = _NEG_BITS ^ 0x7FFFFFFF


def _cparams(n_axes):
    return pltpu.CompilerParams(
        dimension_semantics=("arbitrary",) * n_axes, vmem_limit_bytes=VMEM_LIMIT)


def _dot_nt(a, b):
    return lax.dot_general(a, b, (((1,), (1,)), ((), ())), preferred_element_type=F32)


def _layer_norm(x, g, b):
    mu = jnp.mean(x, axis=-1, keepdims=True)
    xc = x - mu
    var = jnp.mean(xc * xc, axis=-1, keepdims=True)
    return xc * lax.rsqrt(var + LN_EPS) * g + b


def _ln_kernel(x_ref, g_ref, b_ref, o_ref):
    o_ref[...] = _layer_norm(x_ref[...], g_ref[...], b_ref[...])


def _entry_layer_norm(x, g, b, tm=512):
    T, D = x.shape
    row = pl.BlockSpec((tm, D), lambda t: (t, 0))
    vec = pl.BlockSpec((1, D), lambda t: (0, 0))
    return pl.pallas_call(
        _ln_kernel, grid=(T // tm,), in_specs=[row, vec, vec], out_specs=row,
        out_shape=jax.ShapeDtypeStruct((T, D), F32), compiler_params=_cparams(1),
        name="entry_ln")(x, g.reshape(1, D), b.reshape(1, D))


def _inproj_kernel(h_ref, wm_ref, wvt_ref, wiq_ref, wik_ref, wiwt_ref, main_ref, vt_ref, iq_ref, ik_ref, iwt_ref, *, tn):
    hb = h_ref[...].astype(BF16)
    for j in range(0, wm_ref.shape[1], tn):
        main_ref[:, j:j + tn] = jnp.dot(hb, wm_ref[:, j:j + tn], preferred_element_type=F32).astype(BF16)
    vt_ref[0] = _dot_nt(wvt_ref[...], hb).astype(BF16)
    iq_ref[...] = jnp.dot(hb, wiq_ref[...], preferred_element_type=F32) * (IDX_DIM ** -0.5)
    ik_ref[...] = jnp.dot(hb, wik_ref[...], preferred_element_type=F32)
    iwt_ref[...] = _dot_nt(wiwt_ref[...], hb) * (IDX_HEADS ** -0.5)


def _input_projection(h, w_main, w_vt, w_iq, w_ik, w_iwt, tm=512, tn=512):
    T, D = h.shape
    nm = w_main.shape[1]
    nv = w_vt.shape[0]

    def full(w):
        return pl.BlockSpec(w.shape, lambda t: (0, 0))

    def rows(n):
        return pl.BlockSpec((tm, n), lambda t: (t, 0))

    return pl.pallas_call(
        functools.partial(_inproj_kernel, tn=tn), grid=(T // tm,),
        in_specs=[rows(D), full(w_main), full(w_vt), full(w_iq), full(w_ik), full(w_iwt)],
        out_specs=[rows(nm), pl.BlockSpec((1, nv, tm), lambda t: (t, 0, 0)),
                   rows(w_iq.shape[1]), rows(w_ik.shape[1]),
                   pl.BlockSpec((w_iwt.shape[0], tm), lambda t: (0, t))],
        out_shape=[jax.ShapeDtypeStruct((T, nm), BF16),
                   jax.ShapeDtypeStruct((T // tm, nv, tm), BF16),
                   jax.ShapeDtypeStruct((T, w_iq.shape[1]), F32),
                   jax.ShapeDtypeStruct((T, w_ik.shape[1]), F32),
                   jax.ShapeDtypeStruct((w_iwt.shape[0], T), F32)],
        compiler_params=_cparams(1), name="in_proj")(h, w_main, w_vt, w_iq, w_ik, w_iwt)


def _diff_kernel(lamv_ref, g_ref, q_ref, k_ref, v_ref, o_ref, qm_sc, acc_sc, *, tq, tk, lam_init, slopes):
    qi = pl.program_id(1)
    dv = 2 * DIFF_HEAD_DIM
    n_heads = len(slopes)
    lane = lax.broadcasted_iota(I32, (tq, dv), 1)
    for hd in range(n_heads):
        qs = q_ref[0, :, hd * dv:(hd + 1) * dv] * (DIFF_HEAD_DIM ** -0.5)
        zero = jnp.zeros_like(qs)
        qm_sc[2 * hd] = jnp.where(lane < DIFF_HEAD_DIM, qs, zero)
        qm_sc[2 * hd + 1] = jnp.where(lane >= DIFF_HEAD_DIM, qs, zero)
    acc_sc[...] = jnp.zeros_like(acc_sc)
    t_idx = qi * tq + lax.broadcasted_iota(I32, (tq, tk), 0)

    def step(kb, carry, diagonal):
        off = pl.multiple_of(kb * tk, tk)
        s_idx = off + lax.broadcasted_iota(I32, (tq, tk), 1)
        dist = jnp.abs(t_idx - s_idx).astype(F32)
        if diagonal:
            visible = s_idx < (((t_idx >> CHUNK_SHIFT) + 1) << CHUNK_SHIFT)

        def qk(c):
            return _dot_nt(qm_sc[c], k_ref[0, pl.ds(off, tk), (c // 2) * dv:(c // 2 + 1) * dv])

        def pv(c, a, p):
            v = v_ref[0, pl.ds(off, tk), (c // 2) * dv:(c // 2 + 1) * dv]
            acc_sc[c] = a * acc_sc[c] + jnp.dot(p, v, preferred_element_type=F32)

        out = []
        s_next = qk(0)
        pending = None
        for c in range(2 * n_heads):
            s = s_next - slopes[c // 2] * dist
            if c + 1 < 2 * n_heads:
                s_next = qk(c + 1)
            if diagonal:
                s = jnp.where(visible, s, NEG_INF)
            m, l = carry[2 * c], carry[2 * c + 1]
            m_new = jnp.maximum(m, jnp.max(s, axis=-1, keepdims=True))
            p = jnp.exp(s - m_new)
            a = jnp.exp(m - m_new)
            out.extend((m_new, a * l + jnp.sum(p, axis=-1, keepdims=True)))
            if pending is not None:
                pv(*pending)
            pending = (c, a, p.astype(BF16))
        pv(*pending)
        return tuple(out)

    m0 = jnp.full((tq, 1), -jnp.inf, F32)
    l0 = jnp.zeros((tq, 1), F32)
    n_full = (qi * tq) // tk
    carry = lax.fori_loop(0, n_full, functools.partial(step, diagonal=False), (m0, l0) * (2 * n_heads))
    for d in range(max(tq // tk, 1)):
        carry = step(n_full + d, carry, True)

    lv = lamv_ref[...]
    lam = (jnp.exp(jnp.sum(lv[0:1] * lv[1:2], axis=-1, keepdims=True))
           - jnp.exp(jnp.sum(lv[2:3] * lv[3:4], axis=-1, keepdims=True)) + lam_init)
    for hd in range(n_heads):
        o = acc_sc[2 * hd] / carry[4 * hd + 1] - lam * (acc_sc[2 * hd + 1] / carry[4 * hd + 3])
        o = o * lax.rsqrt(jnp.mean(o * o, axis=-1, keepdims=True) + LN_EPS) * g_ref[...]
        o_ref[0, :, hd * dv:(hd + 1) * dv] = (o * (1.0 - lam_init)).astype(o_ref.dtype)


def _diff_attention(main, lam_vecs, subln_g, lam_init, n_heads, col0, tq=256, tk=512):
    B, S, _ = main.shape
    dv = 2 * DIFF_HEAD_DIM
    width = n_heads * dv
    slopes = tuple(2.0 ** (-8.0 * (i + 1) / n_heads) for i in range(n_heads))
    kv = lambda c: pl.BlockSpec((1, S, width), lambda b, q: (b, 0, c))
    qo = lambda c: pl.BlockSpec((1, tq, width), lambda b, q: (b, q, c))
    return pl.pallas_call(
        functools.partial(_diff_kernel, tq=tq, tk=tk, lam_init=lam_init, slopes=slopes),
        grid=(B, S // tq),
        in_specs=[pl.BlockSpec(lam_vecs.shape, lambda b, q: (0, 0)),
                  pl.BlockSpec((1, dv), lambda b, q: (0, 0)),
                  qo(col0), kv(col0 + 1), kv(col0 + 2)],
        out_specs=qo(0),
        out_shape=jax.ShapeDtypeStruct((B, S, width), BF16),
        scratch_shapes=[pltpu.VMEM((2 * n_heads, tq, dv), BF16), pltpu.VMEM((2 * n_heads, tq, dv), F32)],
        compiler_params=_cparams(2), name="diff_attn")(
            lam_vecs, subln_g.reshape(1, dv), main, main, main)


def _sort_key(x):
    bits = pltpu.bitcast(x + 0.0, I32)
    return jnp.where(bits < 0, bits ^ 0x7FFFFFFF, bits)


def _dsa_kernel(q_ref, k_ref, vt_ref, iq_ref, ik_ref, iwt_ref, o_ref, key_sc, mask_sc, qm_sc, acc_sc, *,
                tq, tk, seq, topk, slopes):
    qi = pl.program_id(1)
    q_pos0 = qi * tq
    nkb = (q_pos0 + tq + tk - 1) // tk
    n_tail = seq - nkb * tk
    t_row = q_pos0 + lax.broadcasted_iota(I32, (1, tq), 1)
    chunk_end = ((t_row >> CHUNK_SHIFT) + 1) << CHUNK_SHIFT
    s_col = lax.broadcasted_iota(I32, (tk, tq), 0)

    iq = iq_ref[0].astype(BF16)
    iq_heads = [iq[:, j * IDX_DIM:(j + 1) * IDX_DIM] for j in range(IDX_HEADS)]
    iw_rows = [iwt_ref[j:j + 1, :] for j in range(IDX_HEADS)]

    def score_block(kb, _):
        off = pl.multiple_of(kb * tk, tk)
        ik = ik_ref[0, pl.ds(off, tk), :].astype(BF16)
        acc = jnp.zeros((tk, tq), F32)
        for j in range(IDX_HEADS):
            acc = acc + iw_rows[j] * jnp.maximum(_dot_nt(ik, iq_heads[j]), 0.0)
        key_sc[kb] = _sort_key(jnp.where(off + s_col < chunk_end, acc, NEG_INF))
        return 0

    lax.fori_loop(0, nkb, score_block, 0)

    def count(pred):
        def body(kb, c):
            return c + jnp.sum(pred(key_sc[kb]).astype(I32), axis=0, keepdims=True)
        return lax.fori_loop(0, nkb, body, jnp.zeros((1, tq), I32))

    bits_per_check = 4

    def searching(st):
        i, _, done = st
        return jnp.logical_and(i < 32, jnp.min(done) == 0)

    def bit_steps(st):
        i, r, done = st
        for u in range(bits_per_check):
            cand = jnp.where(done > 0, r, r ^ (jnp.int32(1) << (31 - i - u)))
            cnt = count(lambda kk: kk >= cand) + jnp.where(cand <= KEY_NEG, n_tail, 0)
            r = jnp.where(cnt >= topk, cand, r)
            done = jnp.where(cnt == topk, 1, done)
        return i + bits_per_check, r, done

    _, tau, _ = lax.while_loop(
        searching, bit_steps, (jnp.int32(0), jnp.full((1, tq), -2 ** 31, I32), jnp.zeros((1, tq), I32)))
    cnt_gt = count(lambda kk: kk > tau) + jnp.where(tau < KEY_NEG, n_tail, 0)
    cnt_eq = count(lambda kk: kk == tau) + jnp.where(tau == KEY_NEG, n_tail, 0)
    need = topk - cnt_gt
    has_ties = jnp.max(cnt_eq - need) > 0

    @pl.when(jnp.logical_not(has_ties))
    def _():
        def body(kb, _):
            sel = jnp.logical_and(key_sc[kb] >= tau, kb * tk + s_col < chunk_end)
            mask_sc[kb] = jnp.where(sel, 0.0, NEG_INF)
            return 0
        lax.fori_loop(0, nkb, body, 0)

    @pl.when(has_ties)
    def _():
        r_i = lax.broadcasted_iota(I32, (tk, tk), 0)
        c_i = lax.broadcasted_iota(I32, (tk, tk), 1)
        earlier = jnp.where(c_i < r_i, 1.0, 0.0).astype(BF16)

        def body(kb, seen):
            kk = key_sc[kb]
            eq = kk == tau
            eqf = jnp.where(eq, 1.0, 0.0)
            rank = seen + jnp.dot(earlier, eqf.astype(BF16), preferred_element_type=F32)
            take = jnp.logical_and(eq, rank < need.astype(F32))
            sel = jnp.logical_and(jnp.logical_or(kk > tau, take), kb * tk + s_col < chunk_end)
            mask_sc[kb] = jnp.where(sel, 0.0, NEG_INF)
            return seen + jnp.sum(eqf, axis=0, keepdims=True)
        lax.fori_loop(0, nkb, body, jnp.zeros((1, tq), F32))

    hdim = DSA_HEAD_DIM
    pair = 2 * hdim
    n_heads = len(slopes)
    lane = lax.broadcasted_iota(I32, (tq, pair), 1)
    for p in range(n_heads // 2):
        qs = q_ref[0, :, p * pair:(p + 1) * pair] * (hdim ** -0.5)
        zero = jnp.zeros_like(qs)
        qm_sc[2 * p] = jnp.where(lane < hdim, qs, zero)
        qm_sc[2 * p + 1] = jnp.where(lane >= hdim, qs, zero)
    acc_sc[...] = jnp.zeros_like(acc_sc)

    def attend(kb, carry):
        off = pl.multiple_of(kb * tk, tk)
        dist = jnp.abs(t_row - (off + s_col)).astype(F32)
        madd = mask_sc[kb]

        def qk(hd):
            k = k_ref[0, pl.ds(off, tk), (hd // 2) * pair:(hd // 2 + 1) * pair]
            return _dot_nt(k, qm_sc[hd])

        def pv(hd, a, pr):
            rows = slice(hd * hdim, (hd + 1) * hdim)
            acc_sc[rows, :] = a * acc_sc[rows, :] + jnp.dot(vt_ref[kb, rows, :], pr, preferred_element_type=F32)

        out = []
        s_next = qk(0)
        pending = None
        for hd in range(n_heads):
            s = s_next - slopes[hd] * dist + madd
            if hd + 1 < n_heads:
                s_next = qk(hd + 1)
            m, l = carry[2 * hd], carry[2 * hd + 1]
            m_new = jnp.maximum(m, jnp.max(s, axis=0, keepdims=True))
            pr = jnp.exp(s - m_new)
            a = jnp.exp(m - m_new)
            out.extend((m_new, a * l + jnp.sum(pr, axis=0, keepdims=True)))
            if pending is not None:
                pv(*pending)
            pending = (hd, a, pr.astype(BF16))
        pv(*pending)
        return tuple(out)

    m0 = jnp.full((1, tq), -jnp.inf, F32)
    l0 = jnp.zeros((1, tq), F32)
    carry = lax.fori_loop(0, nkb, attend, (m0, l0) * n_heads)
    for hd in range(n_heads):
        rows = slice(hd * hdim, (hd + 1) * hdim)
        acc_sc[rows, :] = acc_sc[rows, :] / carry[2 * hd + 1]
    o_ref[0] = acc_sc[...].T.astype(o_ref.dtype)


def _dsa_attention(main, vt, iq, ik, iwt, n_heads, col_q, col_k, vrow0, tq=256):
    B, S, _ = main.shape
    tk = vt.shape[-1]
    width = n_heads * DSA_HEAD_DIM
    nq = S // tq
    topk = min(IDX_TOPK_MAX, S // 4)
    slopes = tuple(2.0 ** (-8.0 * (i + 1) / n_heads) for i in range(n_heads))
    qrow = lambda n, c: pl.BlockSpec((1, tq, n), lambda b, q: (b, q, c))
    seqb = lambda n, c: pl.BlockSpec((1, S, n), lambda b, q: (b, 0, c))
    return pl.pallas_call(
        functools.partial(_dsa_kernel, tq=tq, tk=tk, seq=S, topk=topk, slopes=slopes),
        grid=(B, nq),
        in_specs=[qrow(width, col_q), seqb(width, col_k),
                  pl.BlockSpec((S // tk, width, tk), lambda b, q: (b, vrow0, 0)),
                  qrow(iq.shape[-1], 0), seqb(ik.shape[-1], 0),
                  pl.BlockSpec((iwt.shape[0], tq), lambda b, q: (0, b * nq + q))],
        out_specs=qrow(width, 0),
        out_shape=jax.ShapeDtypeStruct((B, S, width), BF16),
        scratch_shapes=[pltpu.VMEM((S // tk, tk, tq), I32), pltpu.VMEM((S // tk, tk, tq), F32),
                        pltpu.VMEM((n_heads, tq, 2 * DSA_HEAD_DIM), BF16),
                        pltpu.VMEM((width, tq), F32)],
        compiler_params=_cparams(2), name="dsa_attn")(main, main, vt, iq, ik, iwt)


def _outproj_kernel(a_ref, b_ref, h_ref, w_ref, g_ref, beta_ref, o_ref, *, alpha):
    wa = a_ref.shape[1]
    y = jnp.dot(a_ref[...], w_ref[:wa, :], preferred_element_type=F32)
    y = y + jnp.dot(b_ref[...], w_ref[wa:, :], preferred_element_type=F32)
    o_ref[...] = _layer_norm(alpha * h_ref[...] + y, g_ref[...], beta_ref[...])


def _output_projection(a, b, h, w_out, g, beta, alpha, tm=512):
    T, D = h.shape
    rows = lambda n: pl.BlockSpec((tm, n), lambda t: (t, 0))
    vec = pl.BlockSpec((1, D), lambda t: (0, 0))
    return pl.pallas_call(
        functools.partial(_outproj_kernel, alpha=alpha), grid=(T // tm,),
        in_specs=[rows(a.shape[1]), rows(b.shape[1]), rows(D),
                  pl.BlockSpec(w_out.shape, lambda t: (0, 0)), vec, vec],
        out_specs=rows(D), out_shape=jax.ShapeDtypeStruct((T, D), F32),
        compiler_params=_cparams(1), name="out_proj_ln1")(
            a, b, h, w_out, g.reshape(1, D), beta.reshape(1, D))


def _top_rows(x, n):
    rows = []
    rank = jnp.full(x.shape, float(n), F32)
    for b in range(n):
        mx = jnp.max(x, axis=0, keepdims=True)
        rows.append(mx)
        hit = x == mx
        rank = jnp.where(hit, float(b), rank)
        x = jnp.where(hit, -jnp.inf, x)
    return rows, rank


def _route_kernel(h_ref, wq_ref, sk_ref, n1_ref, c1_ref, r2_ref, e2_ref):
    hb = h_ref[...].astype(BF16)
    qd = 2 * PEER_HALF_DIM
    k = PEER_TOPK
    for hd in range(PEER_HEADS):
        q = jnp.dot(hb, wq_ref[:, hd * qd:(hd + 1) * qd], preferred_element_type=F32).astype(BF16)
        s1 = _dot_nt(sk_ref[hd, 0], q[:, :PEER_HALF_DIM])
        s2 = _dot_nt(sk_ref[hd, 1], q[:, PEER_HALF_DIM:])
        top1, _ = _top_rows(s1, k)
        top2, rank2 = _top_rows(s2, k)
        top2 = jnp.concatenate(top2, axis=0)
        half = k // 2
        cand = [top1[0] + top2] + [top1[a] + top2[:half] for a in range(1, half)]
        cand.append(jnp.concatenate(top1[half:], axis=0) + top2[0:1])
        best, _ = _top_rows(jnp.concatenate(cand, axis=0), k + 1)
        z = jnp.ones_like(best[0])
        for r in best[1:k]:
            z = z + jnp.exp(r - best[0])
        cut = 0.5 * (best[k - 1] + best[k])
        n1 = jnp.zeros_like(s1)
        for b in range(k):
            n1 = n1 + jnp.where(s1 >= cut - top2[b:b + 1], 1.0, 0.0)
        n1_ref[hd] = n1
        c1_ref[hd] = jnp.where(s1 >= top1[-1], jnp.exp(s1 - top1[0]) / z, 0.0)
        r2_ref[hd] = rank2.astype(BF16)
        e2_ref[hd] = jnp.where(s2 >= top2[k - 1:], jnp.exp(s2 - top2[0:1]), 0.0).astype(BF16)


def _peer_route(h, wq, subkeys, tm=256):
    T, D = h.shape
    shape = (PEER_HEADS, PEER_N_KEYS, T)
    ospec = pl.BlockSpec((PEER_HEADS, PEER_N_KEYS, tm), lambda t: (0, 0, t))
    return pl.pallas_call(
        _route_kernel, grid=(T // tm,),
        in_specs=[pl.BlockSpec((tm, D), lambda t: (t, 0)),
                  pl.BlockSpec(wq.shape, lambda t: (0, 0)),
                  pl.BlockSpec(subkeys.shape, lambda t: (0, 0, 0, 0))],
        out_specs=[ospec] * 4,
        out_shape=[jax.ShapeDtypeStruct(shape, F32), jax.ShapeDtypeStruct(shape, F32),
                   jax.ShapeDtypeStruct(shape, BF16), jax.ShapeDtypeStruct(shape, BF16)],
        compiler_params=_cparams(1), name="peer_route")(h, wq, subkeys)


def _gelu(x):
    return 0.5 * x * (1.0 + lax.erf(x * (2.0 ** -0.5)))


MXU_DIM = 256
GATE_ROWS = 16


def _peer_kernel(h_ref, u_ref, vt_ref, n1_ref, c1_ref, r2_ref, e2_ref, o_ref,
                 xb_sc, acc_sc, part_sc, act_a, act_b, w_a, w_b, r2_sc, e2_sc, *, te, n_tiles):
    s = pl.program_id(1)
    tm, d = h_ref.shape

    @pl.when(s == 0)
    def _():
        xb_sc[...] = h_ref[...].astype(BF16)
        acc_sc[...] = jnp.zeros_like(acc_sc)
        r2_sc[...] = r2_ref[...]
        e2_sc[...] = e2_ref[...]

    per = te // PEER_N_KEYS
    n_k = d // MXU_DIM

    def step(act_new, act_old, w_new, w_old, stages):
        tile = s - 1

        def gate_piece(ii, lt):
            i = tile * per + ii
            lanes = slice(lt * LANES, (lt + 1) * LANES)
            nb = [jnp.broadcast_to(n1_ref[hd, pl.ds(i, 1), :][:, lanes].astype(BF16), (GATE_ROWS, LANES))
                  for hd in range(PEER_HEADS)]
            cb = [jnp.broadcast_to(c1_ref[hd, pl.ds(i, 1), :][:, lanes].astype(BF16), (GATE_ROWS, LANES))
                  for hd in range(PEER_HEADS)]
            zero = jnp.zeros((GATE_ROWS, LANES), BF16)
            for r0 in range(0, PEER_N_KEYS, GATE_ROWS):
                rows = slice(r0, r0 + GATE_ROWS)
                g = zero
                for hd in range(PEER_HEADS):
                    g = g + jnp.where(r2_sc[hd, rows, lanes] < nb[hd], e2_sc[hd, rows, lanes] * cb[hd], zero)
                arow = slice(ii * PEER_N_KEYS + r0, ii * PEER_N_KEYS + r0 + GATE_ROWS)
                w_new[arow, lanes] = g * act_old[arow, lanes]

        def stage1_piece(nc, kc):
            cols = slice(nc * MXU_DIM, (nc + 1) * MXU_DIM)
            kk = slice(kc * MXU_DIM, (kc + 1) * MXU_DIM)
            p = _dot_nt(u_ref[:, kk], xb_sc[cols, kk])
            if kc == 0:
                part_sc[:, cols] = p
            elif kc < n_k - 1:
                part_sc[:, cols] += p
            else:
                act_new[:, cols] = _gelu(part_sc[:, cols] + p).astype(BF16)

        def stage3_piece(nc, ec, mc):
            cols = slice(nc * MXU_DIM, (nc + 1) * MXU_DIM)
            ee = slice(ec * MXU_DIM, (ec + 1) * MXU_DIM)
            rr = slice(mc * (d // 2), (mc + 1) * (d // 2))
            acc_sc[rr, cols] += jnp.dot(vt_ref[rr, ee], w_old[ee, cols], preferred_element_type=F32)

        gates, mxu = [], []
        if 2 in stages:
            gates = [functools.partial(gate_piece, ii, lt) for ii in range(per) for lt in range(tm // LANES)]
        if 1 in stages:
            mxu += [functools.partial(stage1_piece, nc, kc) for nc in range(tm // MXU_DIM) for kc in range(n_k)]
        if 3 in stages:
            mxu += [functools.partial(stage3_piece, nc, ec, mc) for nc in range(tm // MXU_DIM)
                    for ec in range(te // MXU_DIM) for mc in range(2)]
        for b in range(max(len(gates), len(mxu))):
            if b < len(mxu):
                mxu[b]()
            if b < len(gates):
                gates[b]()

    act, w = (act_a, act_b), (w_a, w_b)
    steady = jnp.logical_and(s >= 2, s < n_tiles)
    variants = [(s == 0, 0, (1,)), (s == 1, 1, (1, 2)),
                (jnp.logical_and(steady, s % 2 == 0), 0, (1, 2, 3)),
                (jnp.logical_and(steady, s % 2 == 1), 1, (1, 2, 3)),
                (s == n_tiles, n_tiles % 2, (2, 3)), (s == n_tiles + 1, (n_tiles + 1) % 2, (3,))]
    for cond, par, stages in variants:
        @pl.when(cond)
        def _(par=par, stages=stages):
            step(act[par], act[1 - par], w[1 - par], w[par], stages)

    @pl.when(s == n_tiles + 1)
    def _():
        o_ref[...] = acc_sc[...].T


def _peer_experts(h, u_bf, vt_bf, n1, c1, r2, e2, tm=512, te=512):
    T, D = h.shape
    E = u_bf.shape[0]
    n_tiles = E // te
    rspec = pl.BlockSpec((PEER_HEADS, PEER_N_KEYS, tm), lambda t, s: (0, 0, t))
    return pl.pallas_call(
        functools.partial(_peer_kernel, te=te, n_tiles=n_tiles), grid=(T // tm, n_tiles + 2),
        in_specs=[pl.BlockSpec((tm, D), lambda t, s: (t, 0)),
                  pl.BlockSpec((te, D), lambda t, s: (jnp.minimum(s, n_tiles - 1), 0)),
                  pl.BlockSpec((D, te), lambda t, s: (0, jnp.clip(s - 2, 0, n_tiles - 1))),
                  rspec, rspec, rspec, rspec],
        out_specs=pl.BlockSpec((tm, D), lambda t, s: (t, 0)),
        out_shape=jax.ShapeDtypeStruct((T, D), F32),
        scratch_shapes=[pltpu.VMEM((tm, D), BF16), pltpu.VMEM((D, tm), F32), pltpu.VMEM((te, tm), F32),
                        pltpu.VMEM((te, tm), BF16), pltpu.VMEM((te, tm), BF16),
                        pltpu.VMEM((te, tm), BF16), pltpu.VMEM((te, tm), BF16),
                        pltpu.VMEM((PEER_HEADS, PEER_N_KEYS, tm), BF16), pltpu.VMEM((PEER_HEADS, PEER_N_KEYS, tm), BF16)],
        compiler_params=_cparams(2), name="peer_experts")(h, u_bf, vt_bf, n1, c1, r2, e2)


def _ple_kernel(h_ref, f_ref, p_ref, wg_ref, bg_ref, wp_ref, g_ref, beta_ref, o_ref, *, alpha):
    r = alpha * h_ref[...] + f_ref[...]
    z = jnp.dot(r.astype(BF16), wg_ref[...], preferred_element_type=F32) + bg_ref[...]
    gate = 1.0 / (1.0 + jnp.exp(-z))
    r = r + gate * jnp.dot(p_ref[...].astype(BF16), wp_ref[...], preferred_element_type=F32)
    o_ref[...] = _layer_norm(r, g_ref[...], beta_ref[...])


def _ple_ln2(h, f, p, layer, wg, bg, wp, g, beta, alpha, tm=512):
    T, D = h.shape
    pd = p.shape[-1]
    rows = pl.BlockSpec((tm, D), lambda t: (t, 0))
    vec = pl.BlockSpec((1, D), lambda t: (0, 0))
    return pl.pallas_call(
        functools.partial(_ple_kernel, alpha=alpha), grid=(T // tm,),
        in_specs=[rows, rows, pl.BlockSpec((None, tm, pd), lambda t: (layer, t, 0)),
                  pl.BlockSpec(wg.shape, lambda t: (0, 0)), vec,
                  pl.BlockSpec(wp.shape, lambda t: (0, 0)), vec, vec],
        out_specs=rows, out_shape=jax.ShapeDtypeStruct((T, D), F32),
        compiler_params=_cparams(1), name="ple_ln2")(
            h, f, p, wg, bg.reshape(1, D), wp, g.reshape(1, D), beta.reshape(1, D))


def _mixer_layer(h, B, S, w_in, w_out, lam_vecs, subln_g, lam_init, ln_g, ln_b, alpha):
    T, D = h.shape
    dw = D // 2
    n_diff = dw // (2 * DIFF_HEAD_DIM)
    sw = D - dw
    n_dsa = sw // DSA_HEAD_DIM
    n_iq = IDX_HEADS * IDX_DIM
    wb = w_in.astype(BF16)
    c_sv = 3 * dw + 2 * sw
    c_iq = c_sv + sw
    w_main = wb[:, :c_sv]
    w_vt = wb[:, c_sv:c_iq].T
    w_iq = wb[:, c_iq:c_iq + n_iq]
    w_ik = wb[:, c_iq + n_iq:c_iq + n_iq + IDX_DIM]
    w_iwt = jnp.pad(wb[:, c_iq + n_iq + IDX_DIM:].T, ((0, 8 - IDX_HEADS), (0, 0)))
    main, vt, iq, ik, iwt = _input_projection(h, w_main, w_vt, w_iq, w_ik, w_iwt)
    main = main.reshape(B, S, c_sv)
    a = _diff_attention(main, lam_vecs, subln_g, lam_init, n_diff, 0)
    b = _dsa_attention(main, vt, iq.reshape(B, S, -1), ik.reshape(B, S, -1), iwt, n_dsa,
                       3 * dw // sw, 3 * dw // sw + 1, 0)
    return _output_projection(a.reshape(T, -1), b.reshape(T, -1), h, w_out.astype(BF16), ln_g, ln_b, alpha)


def kernel(x, p, ln_in_g, ln_in_b, w_in, w_out, diff_lambda, diff_subln_g, ln1_g, ln1_b, peer_wq, peer_subkeys, peer_u, peer_v, ple_wg, ple_bg, ple_wp, ln2_g, ln2_b):
    B, S, D = x.shape
    depth = w_in.shape[0]
    T = B * S
    alpha = (2 * depth) ** 0.25
    h = _entry_layer_norm(x.reshape(T, D), ln_in_g, ln_in_b)
    p2 = p.reshape(depth, T, p.shape[-1])
    for i in range(depth):
        lam_init = 0.8 - 0.6 * math.exp(-0.3 * i)
        h = _mixer_layer(h, B, S, w_in[i], w_out[i], diff_lambda[i], diff_subln_g[i], lam_init,
                         ln1_g[i], ln1_b[i], alpha)
        n1, c1, r2, e2 = _peer_route(h, peer_wq[i].astype(BF16), peer_subkeys[i].astype(BF16))
        f = _peer_experts(h, peer_u[i].astype(BF16), peer_v[i].astype(BF16).T, n1, c1, r2, e2)
        h = _ple_ln2(h, f, p2, i, ple_wg[i].astype(BF16), ple_bg[i], ple_wp[i].astype(BF16),
                     ln2_g[i], ln2_b[i], alpha)
    return h.reshape(B, S, D)
```

```python
import functools
import math
import struct

import jax
import jax.numpy as jnp
from jax import lax
from jax.experimental import pallas as pl
from jax.experimental.pallas import tpu as pltpu

F32 = jnp.float32
BF16 = jnp.bfloat16
I32 = jnp.int32

LN_EPS = 1e-5
NEG_INF = -1e30
CHUNK = 64
CHUNK_SHIFT = 6
LANES = 128
VMEM_LIMIT = 56 * 1024 * 1024

DIFF_HEAD_DIM = 64
DSA_HEAD_DIM = 64
IDX_HEADS = 4
IDX_DIM = 64
IDX_TOPK_MAX = 256
PEER_HEADS = 8
PEER_N_KEYS = 128
PEER_HALF_DIM = 128
PEER_TOPK = 16

_NEG_BITS = struct.unpack("<i", struct.pack("<f", NEG_INF))[0]
KEY_NEG = _NEG_BITS ^ 0x7FFFFFFF


def _cparams(n_axes):
    return pltpu.CompilerParams(
        dimension_semantics=("arbitrary",) * n_axes, vmem_limit_bytes=VMEM_LIMIT)


def _dot_nt(a, b):
    return lax.dot_general(a, b, (((1,), (1,)), ((), ())), preferred_element_type=F32)


def _layer_norm(x, g, b):
    mu = jnp.mean(x, axis=-1, keepdims=True)
    xc = x - mu
    var = jnp.mean(xc * xc, axis=-1, keepdims=True)
    return xc * lax.rsqrt(var + LN_EPS) * g + b


def _ln_kernel(x_ref, g_ref, b_ref, o_ref):
    o_ref[...] = _layer_norm(x_ref[...], g_ref[...], b_ref[...])


def _entry_layer_norm(x, g, b, tm=512):
    T, D = x.shape
    row = pl.BlockSpec((tm, D), lambda t: (t, 0))
    vec = pl.BlockSpec((1, D), lambda t: (0, 0))
    return pl.pallas_call(
        _ln_kernel, grid=(T // tm,), in_specs=[row, vec, vec], out_specs=row,
        out_shape=jax.ShapeDtypeStruct((T, D), F32), compiler_params=_cparams(1),
        name="entry_ln")(x, g.reshape(1, D), b.reshape(1, D))


def _inproj_kernel(h_ref, wm_ref, wvt_ref, wiq_ref, wik_ref, wiwt_ref, main_ref, vt_ref, iq_ref, ik_ref, iwt_ref, *, tn):
    hb = h_ref[...].astype(BF16)
    for j in range(0, wm_ref.shape[1], tn):
        main_ref[:, j:j + tn] = jnp.dot(hb, wm_ref[:, j:j + tn], preferred_element_type=F32).astype(BF16)
    vt_ref[0] = _dot_nt(wvt_ref[...], hb).astype(BF16)
    iq_ref[...] = jnp.dot(hb, wiq_ref[...], preferred_element_type=F32) * (IDX_DIM ** -0.5)
    ik_ref[...] = jnp.dot(hb, wik_ref[...], preferred_element_type=F32)
    iwt_ref[...] = _dot_nt(wiwt_ref[...], hb) * (IDX_HEADS ** -0.5)


def _input_projection(h, w_main, w_vt, w_iq, w_ik, w_iwt, tm=512, tn=512):
    T, D = h.shape
    nm = w_main.shape[1]
    nv = w_vt.shape[0]

    def full(w):
        return pl.BlockSpec(w.shape, lambda t: (0, 0))

    def rows(n):
        return pl.BlockSpec((tm, n), lambda t: (t, 0))

    return pl.pallas_call(
        functools.partial(_inproj_kernel, tn=tn), grid=(T // tm,),
        in_specs=[rows(D), full(w_main), full(w_vt), full(w_iq), full(w_ik), full(w_iwt)],
        out_specs=[rows(nm), pl.BlockSpec((1, nv, tm), lambda t: (t, 0, 0)),
                   rows(w_iq.shape[1]), rows(w_ik.shape[1]),
                   pl.BlockSpec((w_iwt.shape[0], tm), lambda t: (0, t))],
        out_shape=[jax.ShapeDtypeStruct((T, nm), BF16),
                   jax.ShapeDtypeStruct((T // tm, nv, tm), BF16),
                   jax.ShapeDtypeStruct((T, w_iq.shape[1]), F32),
                   jax.ShapeDtypeStruct((T, w_ik.shape[1]), F32),
                   jax.ShapeDtypeStruct((w_iwt.shape[0], T), F32)],
        compiler_params=_cparams(1), name="in_proj")(h, w_main, w_vt, w_iq, w_ik, w_iwt)


def _diff_kernel(lamv_ref, g_ref, q_ref, k_ref, v_ref, o_ref, qm_sc, acc_sc, *, tq, tk, lam_init, slopes):
    qi = pl.program_id(1)
    dv = 2 * DIFF_HEAD_DIM
    n_heads = len(slopes)
    lane = lax.broadcasted_iota(I32, (tq, dv), 1)
    for hd in range(n_heads):
        qs = q_ref[0, :, hd * dv:(hd + 1) * dv] * (DIFF_HEAD_DIM ** -0.5)
        zero = jnp.zeros_like(qs)
        qm_sc[2 * hd] = jnp.where(lane < DIFF_HEAD_DIM, qs, zero)
        qm_sc[2 * hd + 1] = jnp.where(lane >= DIFF_HEAD_DIM, qs, zero)
    acc_sc[...] = jnp.zeros_like(acc_sc)
    t_idx = qi * tq + lax.broadcasted_iota(I32, (tq, tk), 0)

    def step(kb, carry, diagonal):
        off = pl.multiple_of(kb * tk, tk)
        s_idx = off + lax.broadcasted_iota(I32, (tq, tk), 1)
        dist = jnp.abs(t_idx - s_idx).astype(F32)
        if diagonal:
            visible = s_idx < (((t_idx >> CHUNK_SHIFT) + 1) << CHUNK_SHIFT)

        def qk(c):
            return _dot_nt(qm_sc[c], k_ref[0, pl.ds(off, tk), (c // 2) * dv:(c // 2 + 1) * dv])

        def pv(c, a, p):
            v = v_ref[0, pl.ds(off, tk), (c // 2) * dv:(c // 2 + 1) * dv]
            acc_sc[c] = a * acc_sc[c] + jnp.dot(p, v, preferred_element_type=F32)

        out = []
        s_next = qk(0)
        pending = None
        for c in range(2 * n_heads):
            s = s_next - slopes[c // 2] * dist
            if c + 1 < 2 * n_heads:
                s_next = qk(c + 1)
            if diagonal:
                s = jnp.where(visible, s, NEG_INF)
            m, l = carry[2 * c], carry[2 * c + 1]
            m_new = jnp.maximum(m, jnp.max(s, axis=-1, keepdims=True))
            p = jnp.exp(s - m_new)
            a = jnp.exp(m - m_new)
            out.extend((m_new, a * l + jnp.sum(p, axis=-1, keepdims=True)))
            if pending is not None:
                pv(*pending)
            pending = (c, a, p.astype(BF16))
        pv(*pending)
        return tuple(out)

    m0 = jnp.full((tq, 1), -jnp.inf, F32)
    l0 = jnp.zeros((tq, 1), F32)
    n_full = (qi * tq) // tk
    carry = lax.fori_loop(0, n_full, functools.partial(step, diagonal=False), (m0, l0) * (2 * n_heads))
    for d in range(max(tq // tk, 1)):
        carry = step(n_full + d, carry, True)

    lv = lamv_ref[...]
    lam = (jnp.exp(jnp.sum(lv[0:1] * lv[1:2], axis=-1, keepdims=True))
           - jnp.exp(jnp.sum(lv[2:3] * lv[3:4], axis=-1, keepdims=True)) + lam_init)
    for hd in range(n_heads):
        o = acc_sc[2 * hd] / carry[4 * hd + 1] - lam * (acc_sc[2 * hd + 1] / carry[4 * hd + 3])
        o = o * lax.rsqrt(jnp.mean(o * o, axis=-1, keepdims=True) + LN_EPS) * g_ref[...]
        o_ref[0, :, hd * dv:(hd + 1) * dv] = (o * (1.0 - lam_init)).astype(o_ref.dtype)


def _diff_attention(main, lam_vecs, subln_g, lam_init, n_heads, col0, tq=256, tk=512):
    B, S, _ = main.shape
    dv = 2 * DIFF_HEAD_DIM
    width = n_heads * dv
    slopes = tuple(2.0 ** (-8.0 * (i + 1) / n_heads) for i in range(n_heads))
    kv = lambda c: pl.BlockSpec((1, S, width), lambda b, q: (b, 0, c))
    qo = lambda c: pl.BlockSpec((1, tq, width), lambda b, q: (b, q, c))
    return pl.pallas_call(
        functools.partial(_diff_kernel, tq=tq, tk=tk, lam_init=lam_init, slopes=slopes),
        grid=(B, S // tq),
        in_specs=[pl.BlockSpec(lam_vecs.shape, lambda b, q: (0, 0)),
                  pl.BlockSpec((1, dv), lambda b, q: (0, 0)),
                  qo(col0), kv(col0 + 1), kv(col0 + 2)],
        out_specs=qo(0),
        out_shape=jax.ShapeDtypeStruct((B, S, width), BF16),
        scratch_shapes=[pltpu.VMEM((2 * n_heads, tq, dv), BF16), pltpu.VMEM((2 * n_heads, tq, dv), F32)],
        compiler_params=_cparams(2), name="diff_attn")(
            lam_vecs, subln_g.reshape(1, dv), main, main, main)


def _sort_key(x):
    bits = pltpu.bitcast(x + 0.0, I32)
    return jnp.where(bits < 0, bits ^ 0x7FFFFFFF, bits)


def _dsa_kernel(q_ref, k_ref, vt_ref, iq_ref, ik_ref, iwt_ref, o_ref, key_sc, mask_sc, qm_sc, acc_sc, *,
                tq, tk, seq, topk, slopes):
    qi = pl.program_id(1)
    q_pos0 = qi * tq
    nkb = (q_pos0 + tq + tk - 1) // tk
    n_tail = seq - nkb * tk
    t_row = q_pos0 + lax.broadcasted_iota(I32, (1, tq), 1)
    chunk_end = ((t_row >> CHUNK_SHIFT) + 1) << CHUNK_SHIFT
    s_col = lax.broadcasted_iota(I32, (tk, tq), 0)

    iq = iq_ref[0].astype(BF16)
    iq_heads = [iq[:, j * IDX_DIM:(j + 1) * IDX_DIM] for j in range(IDX_HEADS)]
    iw_rows = [iwt_ref[j:j + 1, :] for j in range(IDX_HEADS)]

    def score_block(kb, _):
        off = pl.multiple_of(kb * tk, tk)
        ik = ik_ref[0, pl.ds(off, tk), :].astype(BF16)
        acc = jnp.zeros((tk, tq), F32)
        for j in range(IDX_HEADS):
            acc = acc + iw_rows[j] * jnp.maximum(_dot_nt(ik, iq_heads[j]), 0.0)
        key_sc[kb] = _sort_key(jnp.where(off + s_col < chunk_end, acc, NEG_INF))
        return 0

    lax.fori_loop(0, nkb, score_block, 0)

    def count(pred):
        def body(kb, c):
            return c + jnp.sum(pred(key_sc[kb]).astype(I32), axis=0, keepdims=True)
        return lax.fori_loop(0, nkb, body, jnp.zeros((1, tq), I32))

    bits_per_check = 4

    def searching(st):
        i, _, done = st
        return jnp.logical_and(i < 32, jnp.min(done) == 0)

    def bit_steps(st):
        i, r, done = st
        for u in range(bits_per_check):
            cand = jnp.where(done > 0, r, r ^ (jnp.int32(1) << (31 - i - u)))
            cnt = count(lambda kk: kk >= cand) + jnp.where(cand <= KEY_NEG, n_tail, 0)
            r = jnp.where(cnt >= topk, cand, r)
            done = jnp.where(cnt == topk, 1, done)
        return i + bits_per_check, r, done

    _, tau, _ = lax.while_loop(
        searching, bit_steps, (jnp.int32(0), jnp.full((1, tq), -2 ** 31, I32), jnp.zeros((1, tq), I32)))
    cnt_gt = count(lambda kk: kk > tau) + jnp.where(tau < KEY_NEG, n_tail, 0)
    cnt_eq = count(lambda kk: kk == tau) + jnp.where(tau == KEY_NEG, n_tail, 0)
    need = topk - cnt_gt
    has_ties = jnp.max(cnt_eq - need) > 0

    @pl.when(jnp.logical_not(has_ties))
    def _():
        def body(kb, _):
            sel = jnp.logical_and(key_sc[kb] >= tau, kb * tk + s_col < chunk_end)
            mask_sc[kb] = jnp.where(sel, 0.0, NEG_INF)
            return 0
        lax.fori_loop(0, nkb, body, 0)

    @pl.when(has_ties)
    def _():
        r_i = lax.broadcasted_iota(I32, (tk, tk), 0)
        c_i = lax.broadcasted_iota(I32, (tk, tk), 1)
        earlier = jnp.where(c_i < r_i, 1.0, 0.0).astype(BF16)

        def body(kb, seen):
            kk = key_sc[kb]
            eq = kk == tau
            eqf = jnp.where(eq, 1.0, 0.0)
            rank = seen + jnp.dot(earlier, eqf.astype(BF16), preferred_element_type=F32)
            take = jnp.logical_and(eq, rank < need.astype(F32))
            sel = jnp.logical_and(jnp.logical_or(kk > tau, take), kb * tk + s_col < chunk_end)
            mask_sc[kb] = jnp.where(sel, 0.0, NEG_INF)
            return seen + jnp.sum(eqf, axis=0, keepdims=True)
        lax.fori_loop(0, nkb, body, jnp.zeros((1, tq), F32))

    hdim = DSA_HEAD_DIM
    pair = 2 * hdim
    n_heads = len(slopes)
    lane = lax.broadcasted_iota(I32, (tq, pair), 1)
    for p in range(n_heads // 2):
        qs = q_ref[0, :, p * pair:(p + 1) * pair] * (hdim ** -0.5)
        zero = jnp.zeros_like(qs)
        qm_sc[2 * p] = jnp.where(lane < hdim, qs, zero)
        qm_sc[2 * p + 1] = jnp.where(lane >= hdim, qs, zero)
    acc_sc[...] = jnp.zeros_like(acc_sc)

    def attend(kb, carry):
        off = pl.multiple_of(kb * tk, tk)
        dist = jnp.abs(t_row - (off + s_col)).astype(F32)
        madd = mask_sc[kb]

        def qk(hd):
            k = k_ref[0, pl.ds(off, tk), (hd // 2) * pair:(hd // 2 + 1) * pair]
            return _dot_nt(k, qm_sc[hd])

        def pv(hd, a, pr):
            rows = slice(hd * hdim, (hd + 1) * hdim)
            acc_sc[rows, :] = a * acc_sc[rows, :] + jnp.dot(vt_ref[kb, rows, :], pr, preferred_element_type=F32)

        out = []
        s_next = qk(0)
        pending = None
        for hd in range(n_heads):
            s = s_next - slopes[hd] * dist + madd
            if hd + 1 < n_heads:
                s_next = qk(hd + 1)
            m, l = carry[2 * hd], carry[2 * hd + 1]
            m_new = jnp.maximum(m, jnp.max(s, axis=0, keepdims=True))
            pr = jnp.exp(s - m_new)
            a = jnp.exp(m - m_new)
            out.extend((m_new, a * l + jnp.sum(pr, axis=0, keepdims=True)))
            if pending is not None:
                pv(*pending)
            pending = (hd, a, pr.astype(BF16))
        pv(*pending)
        return tuple(out)

    m0 = jnp.full((1, tq), -jnp.inf, F32)
    l0 = jnp.zeros((1, tq), F32)
    carry = lax.fori_loop(0, nkb, attend, (m0, l0) * n_heads)
    for hd in range(n_heads):
        rows = slice(hd * hdim, (hd + 1) * hdim)
        acc_sc[rows, :] = acc_sc[rows, :] / carry[2 * hd + 1]
    o_ref[0] = acc_sc[...].T.astype(o_ref.dtype)


def _dsa_attention(main, vt, iq, ik, iwt, n_heads, col_q, col_k, vrow0, tq=256):
    B, S, _ = main.shape
    tk = vt.shape[-1]
    width = n_heads * DSA_HEAD_DIM
    nq = S // tq
    topk = min(IDX_TOPK_MAX, S // 4)
    slopes = tuple(2.0 ** (-8.0 * (i + 1) / n_heads) for i in range(n_heads))
    qrow = lambda n, c: pl.BlockSpec((1, tq, n), lambda b, q: (b, q, c))
    seqb = lambda n, c: pl.BlockSpec((1, S, n), lambda b, q: (b, 0, c))
    return pl.pallas_call(
        functools.partial(_dsa_kernel, tq=tq, tk=tk, seq=S, topk=topk, slopes=slopes),
        grid=(B, nq),
        in_specs=[qrow(width, col_q), seqb(width, col_k),
                  pl.BlockSpec((S // tk, width, tk), lambda b, q: (b, vrow0, 0)),
                  qrow(iq.shape[-1], 0), seqb(ik.shape[-1], 0),
                  pl.BlockSpec((iwt.shape[0], tq), lambda b, q: (0, b * nq + q))],
        out_specs=qrow(width, 0),
        out_shape=jax.ShapeDtypeStruct((B, S, width), BF16),
        scratch_shapes=[pltpu.VMEM((S // tk, tk, tq), I32), pltpu.VMEM((S // tk, tk, tq), F32),
                        pltpu.VMEM((n_heads, tq, 2 * DSA_HEAD_DIM), BF16),
                        pltpu.VMEM((width, tq), F32)],
        compiler_params=_cparams(2), name="dsa_attn")(main, main, vt, iq, ik, iwt)


def _outproj_kernel(a_ref, b_ref, h_ref, w_ref, g_ref, beta_ref, o_ref, *, alpha):
    wa = a_ref.shape[1]
    y = jnp.dot(a_ref[...], w_ref[:wa, :], preferred_element_type=F32)
    y = y + jnp.dot(b_ref[...], w_ref[wa:, :], preferred_element_type=F32)
    o_ref[...] = _layer_norm(alpha * h_ref[...] + y, g_ref[...], beta_ref[...])


def _output_projection(a, b, h, w_out, g, beta, alpha, tm=512):
    T, D = h.shape
    rows = lambda n: pl.BlockSpec((tm, n), lambda t: (t, 0))
    vec = pl.BlockSpec((1, D), lambda t: (0, 0))
    return pl.pallas_call(
        functools.partial(_outproj_kernel, alpha=alpha), grid=(T // tm,),
        in_specs=[rows(a.shape[1]), rows(b.shape[1]), rows(D),
                  pl.BlockSpec(w_out.shape, lambda t: (0, 0)), vec, vec],
        out_specs=rows(D), out_shape=jax.ShapeDtypeStruct((T, D), F32),
        compiler_params=_cparams(1), name="out_proj_ln1")(
            a, b, h, w_out, g.reshape(1, D), beta.reshape(1, D))


def _top_rows(x, n):
    rows = []
    rank = jnp.full(x.shape, float(n), F32)
    for b in range(n):
        mx = jnp.max(x, axis=0, keepdims=True)
        rows.append(mx)
        hit = x == mx
        rank = jnp.where(hit, float(b), rank)
        x = jnp.where(hit, -jnp.inf, x)
    return rows, rank


def _route_kernel(h_ref, wq_ref, sk_ref, n1_ref, c1_ref, r2_ref, e2_ref):
    hb = h_ref[...].astype(BF16)
    qd = 2 * PEER_HALF_DIM
    k = PEER_TOPK
    for hd in range(PEER_HEADS):
        q = jnp.dot(hb, wq_ref[:, hd * qd:(hd + 1) * qd], preferred_element_type=F32).astype(BF16)
        s1 = _dot_nt(sk_ref[hd, 0], q[:, :PEER_HALF_DIM])
        s2 = _dot_nt(sk_ref[hd, 1], q[:, PEER_HALF_DIM:])
        top1, _ = _top_rows(s1, k)
        top2, rank2 = _top_rows(s2, k)
        top2 = jnp.concatenate(top2, axis=0)
        half = k // 2
        cand = [top1[0] + top2] + [top1[a] + top2[:half] for a in range(1, half)]
        cand.append(jnp.concatenate(top1[half:], axis=0) + top2[0:1])
        best, _ = _top_rows(jnp.concatenate(cand, axis=0), k + 1)
        z = jnp.ones_like(best[0])
        for r in best[1:k]:
            z = z + jnp.exp(r - best[0])
        cut = 0.5 * (best[k - 1] + best[k])
        n1 = jnp.zeros_like(s1)
        for b in range(k):
            n1 = n1 + jnp.where(s1 >= cut - top2[b:b + 1], 1.0, 0.0)
        n1_ref[hd] = n1
        c1_ref[hd] = jnp.where(s1 >= top1[-1], jnp.exp(s1 - top1[0]) / z, 0.0)
        r2_ref[hd] = rank2.astype(BF16)
        e2_ref[hd] = jnp.where(s2 >= top2[k - 1:], jnp.exp(s2 - top2[0:1]), 0.0).astype(BF16)


def _peer_route(h, wq, subkeys, tm=256):
    T, D = h.shape
    shape = (PEER_HEADS, PEER_N_KEYS, T)
    ospec = pl.BlockSpec((PEER_HEADS, PEER_N_KEYS, tm), lambda t: (0, 0, t))
    return pl.pallas_call(
        _route_kernel, grid=(T // tm,),
        in_specs=[pl.BlockSpec((tm, D), lambda t: (t, 0)),
                  pl.BlockSpec(wq.shape, lambda t: (0, 0)),
                  pl.BlockSpec(subkeys.shape, lambda t: (0, 0, 0, 0))],
        out_specs=[ospec] * 4,
        out_shape=[jax.ShapeDtypeStruct(shape, F32), jax.ShapeDtypeStruct(shape, F32),
                   jax.ShapeDtypeStruct(shape, BF16), jax.ShapeDtypeStruct(shape, BF16)],
        compiler_params=_cparams(1), name="peer_route")(h, wq, subkeys)


def _gelu(x):
    return 0.5 * x * (1.0 + lax.erf(x * (2.0 ** -0.5)))


PEER_TILE = 512
MXU_DIM = 256
GATE_ROWS = 16


def _peer_kernel(h_ref, u_ref, vt_ref, n1_ref, c1_ref, r2_ref, e2_ref, o_ref,
                 xb_sc, acc_sc, part_sc, act_a, act_b, w_a, w_b, r2_sc, e2_sc, *, te, n_tiles):
    s = pl.program_id(1)
    tm, d = h_ref.shape

    @pl.when(s == 0)
    def _():
        xb_sc[...] = h_ref[...].astype(BF16)
        acc_sc[...] = jnp.zeros_like(acc_sc)
        r2_sc[...] = r2_ref[...]
        e2_sc[...] = e2_ref[...]

    per = te // PEER_N_KEYS
    n_k = d // MXU_DIM

    def step(act_new, act_old, w_new, w_old, stages):
        tile = s - 1

        def gate_piece(ii, lt):
            i = tile * per + ii
            lanes = slice(lt * LANES, (lt + 1) * LANES)
            nb = [jnp.broadcast_to(n1_ref[hd, pl.ds(i, 1), :][:, lanes].astype(BF16), (GATE_ROWS, LANES))
                  for hd in range(PEER_HEADS)]
            cb = [jnp.broadcast_to(c1_ref[hd, pl.ds(i, 1), :][:, lanes].astype(BF16), (GATE_ROWS, LANES))
                  for hd in range(PEER_HEADS)]
            zero = jnp.zeros((GATE_ROWS, LANES), BF16)
            for r0 in range(0, PEER_N_KEYS, GATE_ROWS):
                rows = slice(r0, r0 + GATE_ROWS)
                g = zero
                for hd in range(PEER_HEADS):
                    g = g + jnp.where(r2_sc[hd, rows, lanes] < nb[hd], e2_sc[hd, rows, lanes] * cb[hd], zero)
                arow = slice(ii * PEER_N_KEYS + r0, ii * PEER_N_KEYS + r0 + GATE_ROWS)
                w_new[arow, lanes] = g * act_old[arow, lanes]

        def stage1_piece(nc, kc):
            cols = slice(nc * MXU_DIM, (nc + 1) * MXU_DIM)
            kk = slice(kc * MXU_DIM, (kc + 1) * MXU_DIM)
            p = _dot_nt(u_ref[:, kk], xb_sc[cols, kk])
            if kc == 0:
                part_sc[:, cols] = p
            elif kc < n_k - 1:
                part_sc[:, cols] += p
            else:
                act_new[:, cols] = _gelu(part_sc[:, cols] + p).astype(BF16)

        def stage3_piece(nc, ec, mc):
            cols = slice(nc * MXU_DIM, (nc + 1) * MXU_DIM)
            ee = slice(ec * MXU_DIM, (ec + 1) * MXU_DIM)
            rr = slice(mc * (d // 2), (mc + 1) * (d // 2))
            acc_sc[rr, cols] += jnp.dot(vt_ref[rr, ee], w_old[ee, cols], preferred_element_type=F32)

        gates, mxu = [], []
        if 2 in stages:
            gates = [functools.partial(gate_piece, ii, lt) for ii in range(per) for lt in range(tm // LANES)]
        if 1 in stages:
            mxu += [functools.partial(stage1_piece, nc, kc) for nc in range(tm // MXU_DIM) for kc in range(n_k)]
        if 3 in stages:
            mxu += [functools.partial(stage3_piece, nc, ec, mc) for nc in range(tm // MXU_DIM)
                    for ec in range(te // MXU_DIM) for mc in range(2)]
        for b in range(max(len(gates), len(mxu))):
            if b < len(mxu):
                mxu[b]()
            if b < len(gates):
                gates[b]()

    act, w = (act_a, act_b), (w_a, w_b)
    steady = jnp.logical_and(s >= 2, s < n_tiles)
    variants = [(s == 0, 0, (1,)), (s == 1, 1, (1, 2)),
                (jnp.logical_and(steady, s % 2 == 0), 0, (1, 2, 3)),
                (jnp.logical_and(steady, s % 2 == 1), 1, (1, 2, 3)),
                (s == n_tiles, n_tiles % 2, (2, 3)), (s == n_tiles + 1, (n_tiles + 1) % 2, (3,))]
    for cond, par, stages in variants:
        @pl.when(cond)
        def _(par=par, stages=stages):
            step(act[par], act[1 - par], w[1 - par], w[par], stages)

    @pl.when(s == n_tiles + 1)
    def _():
        o_ref[...] = acc_sc[...].T


def _peer_experts(h, u_bf, vt_tiles, n1, c1, r2, e2, tm=512):
    T, D = h.shape
    n_tiles, _, te = vt_tiles.shape
    rspec = pl.BlockSpec((PEER_HEADS, PEER_N_KEYS, tm), lambda t, s: (0, 0, t))
    return pl.pallas_call(
        functools.partial(_peer_kernel, te=te, n_tiles=n_tiles), grid=(T // tm, n_tiles + 2),
        in_specs=[pl.BlockSpec((tm, D), lambda t, s: (t, 0)),
                  pl.BlockSpec((te, D), lambda t, s: (jnp.minimum(s, n_tiles - 1), 0)),
                  pl.BlockSpec((None, D, te), lambda t, s: (jnp.clip(s - 2, 0, n_tiles - 1), 0, 0)),
                  rspec, rspec, rspec, rspec],
        out_specs=pl.BlockSpec((tm, D), lambda t, s: (t, 0)),
        out_shape=jax.ShapeDtypeStruct((T, D), F32),
        scratch_shapes=[pltpu.VMEM((tm, D), BF16), pltpu.VMEM((D, tm), F32), pltpu.VMEM((te, tm), F32),
                        pltpu.VMEM((te, tm), BF16), pltpu.VMEM((te, tm), BF16),
                        pltpu.VMEM((te, tm), BF16), pltpu.VMEM((te, tm), BF16),
                        pltpu.VMEM((PEER_HEADS, PEER_N_KEYS, tm), BF16), pltpu.VMEM((PEER_HEADS, PEER_N_KEYS, tm), BF16)],
        compiler_params=_cparams(2), name="peer_experts")(h, u_bf, vt_tiles, n1, c1, r2, e2)


def _ple_kernel(h_ref, f_ref, p_ref, wg_ref, bg_ref, wp_ref, g_ref, beta_ref, o_ref, *, alpha):
    r = alpha * h_ref[...] + f_ref[...]
    z = jnp.dot(r.astype(BF16), wg_ref[...], preferred_element_type=F32) + bg_ref[...]
    gate = 1.0 / (1.0 + jnp.exp(-z))
    r = r + gate * jnp.dot(p_ref[...].astype(BF16), wp_ref[...], preferred_element_type=F32)
    o_ref[...] = _layer_norm(r, g_ref[...], beta_ref[...])


def _ple_ln2(h, f, p, layer, wg, bg, wp, g, beta, alpha, tm=512):
    T, D = h.shape
    pd = p.shape[-1]
    rows = pl.BlockSpec((tm, D), lambda t: (t, 0))
    vec = pl.BlockSpec((1, D), lambda t: (0, 0))
    return pl.pallas_call(
        functools.partial(_ple_kernel, alpha=alpha), grid=(T // tm,),
        in_specs=[rows, rows, pl.BlockSpec((None, tm, pd), lambda t: (layer, t, 0)),
                  pl.BlockSpec(wg.shape, lambda t: (0, 0)), vec,
                  pl.BlockSpec(wp.shape, lambda t: (0, 0)), vec, vec],
        out_specs=rows, out_shape=jax.ShapeDtypeStruct((T, D), F32),
        compiler_params=_cparams(1), name="ple_ln2")(
            h, f, p, wg, bg.reshape(1, D), wp, g.reshape(1, D), beta.reshape(1, D))


def _mixer_layer(h, B, S, w_in, w_out, lam_vecs, subln_g, lam_init, ln_g, ln_b, alpha):
    T, D = h.shape
    dw = D // 2
    n_diff = dw // (2 * DIFF_HEAD_DIM)
    sw = D - dw
    n_dsa = sw // DSA_HEAD_DIM
    n_iq = IDX_HEADS * IDX_DIM
    wb = w_in.astype(BF16)
    c_sv = 3 * dw + 2 * sw
    c_iq = c_sv + sw
    w_main = wb[:, :c_sv]
    w_vt = wb[:, c_sv:c_iq].T
    w_iq = wb[:, c_iq:c_iq + n_iq]
    w_ik = wb[:, c_iq + n_iq:c_iq + n_iq + IDX_DIM]
    w_iwt = jnp.pad(wb[:, c_iq + n_iq + IDX_DIM:].T, ((0, 8 - IDX_HEADS), (0, 0)))
    main, vt, iq, ik, iwt = _input_projection(h, w_main, w_vt, w_iq, w_ik, w_iwt)
    main = main.reshape(B, S, c_sv)
    a = _diff_attention(main, lam_vecs, subln_g, lam_init, n_diff, 0)
    b = _dsa_attention(main, vt, iq.reshape(B, S, -1), ik.reshape(B, S, -1), iwt, n_dsa,
                       3 * dw // sw, 3 * dw // sw + 1, 0)
    return _output_projection(a.reshape(T, -1), b.reshape(T, -1), h, w_out.astype(BF16), ln_g, ln_b, alpha)


def kernel(x, p, ln_in_g, ln_in_b, w_in, w_out, diff_lambda, diff_subln_g, ln1_g, ln1_b, peer_wq, peer_subkeys, peer_u, peer_v, ple_wg, ple_bg, ple_wp, ln2_g, ln2_b):
    B, S, D = x.shape
    depth = w_in.shape[0]
    T = B * S
    alpha = (2 * depth) ** 0.25
    h = _entry_layer_norm(x.reshape(T, D), ln_in_g, ln_in_b)
    p2 = p.reshape(depth, T, p.shape[-1])
    for i in range(depth):
        lam_init = 0.8 - 0.6 * math.exp(-0.3 * i)
        h = _mixer_layer(h, B, S, w_in[i], w_out[i], diff_lambda[i], diff_subln_g[i], lam_init,
                         ln1_g[i], ln1_b[i], alpha)
        n1, c1, r2, e2 = _peer_route(h, peer_wq[i].astype(BF16), peer_subkeys[i].astype(BF16))
        vt_tiles = peer_v[i].astype(BF16).reshape(-1, PEER_TILE, D).transpose(0, 2, 1)
        f = _peer_experts(h, peer_u[i].astype(BF16), vt_tiles, n1, c1, r2, e2)
        h = _ple_ln2(h, f, p2, i, ple_wg[i].astype(BF16), ple_bg[i], ple_wp[i].astype(BF16),
                     ln2_g[i], ln2_b[i], alpha)
    return h.reshape(B, S, D)
```

```python
import functools
import math
import struct

import jax
import jax.numpy as jnp
from jax import lax
from jax.experimental import pallas as pl
from jax.experimental.pallas import tpu as pltpu

F32 = jnp.float32
BF16 = jnp.bfloat16
I32 = jnp.int32

LN_EPS = 1e-5
NEG_INF = -1e30
CHUNK = 64
CHUNK_SHIFT = 6
LANES = 128
VMEM_LIMIT = 56 * 1024 * 1024

DIFF_HEAD_DIM = 64
DSA_HEAD_DIM = 64
IDX_HEADS = 4
IDX_DIM = 64
IDX_TOPK_MAX = 256
PEER_HEADS = 8
PEER_N_KEYS = 128
PEER_HALF_DIM = 128
PEER_TOPK = 16

_NEG_BITS = struct.unpack("<i", struct.pack("<f", NEG_INF))[0]
KEY_NEG = _NEG_BITS ^ 0x7FFFFFFF


def _cparams(n_axes):
    return pltpu.CompilerParams(
        dimension_semantics=("arbitrary",) * n_axes, vmem_limit_bytes=VMEM_LIMIT)


def _dot_nt(a, b):
    return lax.dot_general(a, b, (((1,), (1,)), ((), ())), preferred_element_type=F32)


def _layer_norm(x, g, b):
    mu = jnp.mean(x, axis=-1, keepdims=True)
    xc = x - mu
    var = jnp.mean(xc * xc, axis=-1, keepdims=True)
    return xc * lax.rsqrt(var + LN_EPS) * g + b


def _ln_kernel(x_ref, g_ref, b_ref, o_ref):
    o_ref[...] = _layer_norm(x_ref[...], g_ref[...], b_ref[...])


def _entry_layer_norm(x, g, b, tm=512):
    T, D = x.shape
    row = pl.BlockSpec((tm, D), lambda t: (t, 0))
    vec = pl.BlockSpec((1, D), lambda t: (0, 0))
    return pl.pallas_call(
        _ln_kernel, grid=(T // tm,), in_specs=[row, vec, vec], out_specs=row,
        out_shape=jax.ShapeDtypeStruct((T, D), F32), compiler_params=_cparams(1),
        name="entry_ln")(x, g.reshape(1, D), b.reshape(1, D))


def _inproj_kernel(h_ref, wm_ref, wvt_ref, wiq_ref, wik_ref, wiwt_ref, main_ref, vt_ref, iq_ref, ik_ref, iwt_ref, *, tn):
    hb = h_ref[...].astype(BF16)
    for j in range(0, wm_ref.shape[1], tn):
        main_ref[:, j:j + tn] = jnp.dot(hb, wm_ref[:, j:j + tn], preferred_element_type=F32).astype(BF16)
    vt_ref[0] = _dot_nt(wvt_ref[...], hb).astype(BF16)
    iq_ref[...] = jnp.dot(hb, wiq_ref[...], preferred_element_type=F32) * (IDX_DIM ** -0.5)
    ik_ref[...] = jnp.dot(hb, wik_ref[...], preferred_element_type=F32)
    iwt_ref[...] = _dot_nt(wiwt_ref[...], hb) * (IDX_HEADS ** -0.5)


def _input_projection(h, w_main, w_vt, w_iq, w_ik, w_iwt, tm=512, tn=512):
    T, D = h.shape
    nm = w_main.shape[1]
    nv = w_vt.shape[0]

    def full(w):
        return pl.BlockSpec(w.shape, lambda t: (0, 0))

    def rows(n):
        return pl.BlockSpec((tm, n), lambda t: (t, 0))

    return pl.pallas_call(
        functools.partial(_inproj_kernel, tn=tn), grid=(T // tm,),
        in_specs=[rows(D), full(w_main), full(w_vt), full(w_iq), full(w_ik), full(w_iwt)],
        out_specs=[rows(nm), pl.BlockSpec((1, nv, tm), lambda t: (t, 0, 0)),
                   rows(w_iq.shape[1]), rows(w_ik.shape[1]),
                   pl.BlockSpec((w_iwt.shape[0], tm), lambda t: (0, t))],
        out_shape=[jax.ShapeDtypeStruct((T, nm), BF16),
                   jax.ShapeDtypeStruct((T // tm, nv, tm), BF16),
                   jax.ShapeDtypeStruct((T, w_iq.shape[1]), F32),
                   jax.ShapeDtypeStruct((T, w_ik.shape[1]), F32),
                   jax.ShapeDtypeStruct((w_iwt.shape[0], T), F32)],
        compiler_params=_cparams(1), name="in_proj")(h, w_main, w_vt, w_iq, w_ik, w_iwt)


def _diff_kernel(lamv_ref, g_ref, q_ref, k_ref, v_ref, o_ref, qm_sc, acc_sc, *, tq, tk, lam_init, slopes):
    qi = pl.program_id(1)
    dv = 2 * DIFF_HEAD_DIM
    n_heads = len(slopes)
    lane = lax.broadcasted_iota(I32, (tq, dv), 1)
    for hd in range(n_heads):
        qs = q_ref[0, :, hd * dv:(hd + 1) * dv] * (DIFF_HEAD_DIM ** -0.5)
        zero = jnp.zeros_like(qs)
        qm_sc[2 * hd] = jnp.where(lane < DIFF_HEAD_DIM, qs, zero)
        qm_sc[2 * hd + 1] = jnp.where(lane >= DIFF_HEAD_DIM, qs, zero)
    acc_sc[...] = jnp.zeros_like(acc_sc)
    t_idx = qi * tq + lax.broadcasted_iota(I32, (tq, tk), 0)

    def step(kb, carry, diagonal):
        off = pl.multiple_of(kb * tk, tk)
        s_idx = off + lax.broadcasted_iota(I32, (tq, tk), 1)
        dist = jnp.abs(t_idx - s_idx).astype(F32)
        if diagonal:
            visible = s_idx < (((t_idx >> CHUNK_SHIFT) + 1) << CHUNK_SHIFT)

        def qk(c):
            return _dot_nt(qm_sc[c], k_ref[0, pl.ds(off, tk), (c // 2) * dv:(c // 2 + 1) * dv])

        def pv(c, a, p):
            v = v_ref[0, pl.ds(off, tk), (c // 2) * dv:(c // 2 + 1) * dv]
            acc_sc[c] = a * acc_sc[c] + jnp.dot(p, v, preferred_element_type=F32)

        out = []
        s_next = qk(0)
        pending = None
        for c in range(2 * n_heads):
            s = s_next - slopes[c // 2] * dist
            if c + 1 < 2 * n_heads:
                s_next = qk(c + 1)
            if diagonal:
                s = jnp.where(visible, s, NEG_INF)
            m, l = carry[2 * c], carry[2 * c + 1]
            m_new = jnp.maximum(m, jnp.max(s, axis=-1, keepdims=True))
            p = jnp.exp(s - m_new)
            a = jnp.exp(m - m_new)
            out.extend((m_new, a * l + jnp.sum(p, axis=-1, keepdims=True)))
            if pending is not None:
                pv(*pending)
            pending = (c, a, p.astype(BF16))
        pv(*pending)
        return tuple(out)

    m0 = jnp.full((tq, 1), -jnp.inf, F32)
    l0 = jnp.zeros((tq, 1), F32)
    n_full = (qi * tq) // tk
    carry = lax.fori_loop(0, n_full, functools.partial(step, diagonal=False), (m0, l0) * (2 * n_heads))
    for d in range(max(tq // tk, 1)):
        carry = step(n_full + d, carry, True)

    lv = lamv_ref[...]
    lam = (jnp.exp(jnp.sum(lv[0:1] * lv[1:2], axis=-1, keepdims=True))
           - jnp.exp(jnp.sum(lv[2:3] * lv[3:4], axis=-1, keepdims=True)) + lam_init)
    for hd in range(n_heads):
        o = acc_sc[2 * hd] / carry[4 * hd + 1] - lam * (acc_sc[2 * hd + 1] / carry[4 * hd + 3])
        o = o * lax.rsqrt(jnp.mean(o * o, axis=-1, keepdims=True) + LN_EPS) * g_ref[...]
        o_ref[0, :, hd * dv:(hd + 1) * dv] = (o * (1.0 - lam_init)).astype(o_ref.dtype)


def _diff_attention(main, lam_vecs, subln_g, lam_init, n_heads, col0, tq=256, tk=512):
    B, S, _ = main.shape
    dv = 2 * DIFF_HEAD_DIM
    width = n_heads * dv
    slopes = tuple(2.0 ** (-8.0 * (i + 1) / n_heads) for i in range(n_heads))
    kv = lambda c: pl.BlockSpec((1, S, width), lambda b, q: (b, 0, c))
    qo = lambda c: pl.BlockSpec((1, tq, width), lambda b, q: (b, q, c))
    return pl.pallas_call(
        functools.partial(_diff_kernel, tq=tq, tk=tk, lam_init=lam_init, slopes=slopes),
        grid=(B, S // tq),
        in_specs=[pl.BlockSpec(lam_vecs.shape, lambda b, q: (0, 0)),
                  pl.BlockSpec((1, dv), lambda b, q: (0, 0)),
                  qo(col0), kv(col0 + 1), kv(col0 + 2)],
        out_specs=qo(0),
        out_shape=jax.ShapeDtypeStruct((B, S, width), BF16),
        scratch_shapes=[pltpu.VMEM((2 * n_heads, tq, dv), BF16), pltpu.VMEM((2 * n_heads, tq, dv), F32)],
        compiler_params=_cparams(2), name="diff_attn")(
            lam_vecs, subln_g.reshape(1, dv), main, main, main)


def _sort_key(x):
    bits = pltpu.bitcast(x + 0.0, I32)
    return jnp.where(bits < 0, bits ^ 0x7FFFFFFF, bits)


def _dsa_kernel(q_ref, k_ref, vt_ref, iq_ref, ik_ref, iwt_ref, o_ref, key_sc, mask_sc, qm_sc, acc_sc, *,
                tq, tk, seq, topk, slopes):
    qi = pl.program_id(1)
    q_pos0 = qi * tq
    nkb = (q_pos0 + tq + tk - 1) // tk
    n_tail = seq - nkb * tk
    t_row = q_pos0 + lax.broadcasted_iota(I32, (1, tq), 1)
    chunk_end = ((t_row >> CHUNK_SHIFT) + 1) << CHUNK_SHIFT
    s_col = lax.broadcasted_iota(I32, (tk, tq), 0)

    iq = iq_ref[0].astype(BF16)
    iq_heads = [iq[:, j * IDX_DIM:(j + 1) * IDX_DIM] for j in range(IDX_HEADS)]
    iw_rows = [iwt_ref[j:j + 1, :] for j in range(IDX_HEADS)]

    def score_block(kb, _):
        off = pl.multiple_of(kb * tk, tk)
        ik = ik_ref[0, pl.ds(off, tk), :].astype(BF16)
        acc = jnp.zeros((tk, tq), F32)
        for j in range(IDX_HEADS):
            acc = acc + iw_rows[j] * jnp.maximum(_dot_nt(ik, iq_heads[j]), 0.0)
        key_sc[kb] = _sort_key(jnp.where(off + s_col < chunk_end, acc, NEG_INF))
        return 0

    lax.fori_loop(0, nkb, score_block, 0)

    def count(pred):
        def body(kb, c):
            return c + jnp.sum(pred(key_sc[kb]).astype(I32), axis=0, keepdims=True)
        return lax.fori_loop(0, nkb, body, jnp.zeros((1, tq), I32))

    bits_per_check = 4

    def searching(st):
        i, _, done = st
        return jnp.logical_and(i < 32, jnp.min(done) == 0)

    def bit_steps(st):
        i, r, done = st
        for u in range(bits_per_check):
            cand = jnp.where(done > 0, r, r ^ (jnp.int32(1) << (31 - i - u)))
            cnt = count(lambda kk: kk >= cand) + jnp.where(cand <= KEY_NEG, n_tail, 0)
            r = jnp.where(cnt >= topk, cand, r)
            done = jnp.where(cnt == topk, 1, done)
        return i + bits_per_check, r, done

    _, tau, _ = lax.while_loop(
        searching, bit_steps, (jnp.int32(0), jnp.full((1, tq), -2 ** 31, I32), jnp.zeros((1, tq), I32)))
    cnt_gt = count(lambda kk: kk > tau) + jnp.where(tau < KEY_NEG, n_tail, 0)
    cnt_eq = count(lambda kk: kk == tau) + jnp.where(tau == KEY_NEG, n_tail, 0)
    need = topk - cnt_gt
    has_ties = jnp.max(cnt_eq - need) > 0

    @pl.when(jnp.logical_not(has_ties))
    def _():
        def body(kb, _):
            sel = jnp.logical_and(key_sc[kb] >= tau, kb * tk + s_col < chunk_end)
            mask_sc[kb] = jnp.where(sel, 0.0, NEG_INF)
            return 0
        lax.fori_loop(0, nkb, body, 0)

    @pl.when(has_ties)
    def _():
        r_i = lax.broadcasted_iota(I32, (tk, tk), 0)
        c_i = lax.broadcasted_iota(I32, (tk, tk), 1)
        earlier = jnp.where(c_i < r_i, 1.0, 0.0).astype(BF16)

        def body(kb, seen):
            kk = key_sc[kb]
            eq = kk == tau
            eqf = jnp.where(eq, 1.0, 0.0)
            rank = seen + jnp.dot(earlier, eqf.astype(BF16), preferred_element_type=F32)
            take = jnp.logical_and(eq, rank < need.astype(F32))
            sel = jnp.logical_and(jnp.logical_or(kk > tau, take), kb * tk + s_col < chunk_end)
            mask_sc[kb] = jnp.where(sel, 0.0, NEG_INF)
            return seen + jnp.sum(eqf, axis=0, keepdims=True)
        lax.fori_loop(0, nkb, body, jnp.zeros((1, tq), F32))

    hdim = DSA_HEAD_DIM
    pair = 2 * hdim
    n_heads = len(slopes)
    lane = lax.broadcasted_iota(I32, (tq, pair), 1)
    for p in range(n_heads // 2):
        qs = q_ref[0, :, p * pair:(p + 1) * pair] * (hdim ** -0.5)
        zero = jnp.zeros_like(qs)
        qm_sc[2 * p] = jnp.where(lane < hdim, qs, zero)
        qm_sc[2 * p + 1] = jnp.where(lane >= hdim, qs, zero)
    acc_sc[...] = jnp.zeros_like(acc_sc)

    def attend(kb, carry):
        off = pl.multiple_of(kb * tk, tk)
        dist = jnp.abs(t_row - (off + s_col)).astype(F32)
        madd = mask_sc[kb]

        def qk(hd):
            k = k_ref[0, pl.ds(off, tk), (hd // 2) * pair:(hd // 2 + 1) * pair]
            return _dot_nt(k, qm_sc[hd])

        def pv(hd, a, pr):
            rows = slice(hd * hdim, (hd + 1) * hdim)
            acc_sc[rows, :] = a * acc_sc[rows, :] + jnp.dot(vt_ref[kb, rows, :], pr, preferred_element_type=F32)

        out = []
        s_next = qk(0)
        pending = None
        for hd in range(n_heads):
            s = s_next - slopes[hd] * dist + madd
            if hd + 1 < n_heads:
                s_next = qk(hd + 1)
            m, l = carry[2 * hd], carry[2 * hd + 1]
            m_new = jnp.maximum(m, jnp.max(s, axis=0, keepdims=True))
            pr = jnp.exp(s - m_new)
            a = jnp.exp(m - m_new)
            out.extend((m_new, a * l + jnp.sum(pr, axis=0, keepdims=True)))
            if pending is not None:
                pv(*pending)
            pending = (hd, a, pr.astype(BF16))
        pv(*pending)
        return tuple(out)

    m0 = jnp.full((1, tq), -jnp.inf, F32)
    l0 = jnp.zeros((1, tq), F32)
    carry = lax.fori_loop(0, nkb, attend, (m0, l0) * n_heads)
    for hd in range(n_heads):
        rows = slice(hd * hdim, (hd + 1) * hdim)
        acc_sc[rows, :] = acc_sc[rows, :] / carry[2 * hd + 1]
    o_ref[0] = acc_sc[...].T.astype(o_ref.dtype)


def _dsa_attention(main, vt, iq, ik, iwt, n_heads, col_q, col_k, vrow0, tq=256):
    B, S, _ = main.shape
    tk = vt.shape[-1]
    width = n_heads * DSA_HEAD_DIM
    nq = S // tq
    topk = min(IDX_TOPK_MAX, S // 4)
    slopes = tuple(2.0 ** (-8.0 * (i + 1) / n_heads) for i in range(n_heads))
    qrow = lambda n, c: pl.BlockSpec((1, tq, n), lambda b, q: (b, q, c))
    seqb = lambda n, c: pl.BlockSpec((1, S, n), lambda b, q: (b, 0, c))
    return pl.pallas_call(
        functools.partial(_dsa_kernel, tq=tq, tk=tk, seq=S, topk=topk, slopes=slopes),
        grid=(B, nq),
        in_specs=[qrow(width, col_q), seqb(width, col_k),
                  pl.BlockSpec((S // tk, width, tk), lambda b, q: (b, vrow0, 0)),
                  qrow(iq.shape[-1], 0), seqb(ik.shape[-1], 0),
                  pl.BlockSpec((iwt.shape[0], tq), lambda b, q: (0, b * nq + q))],
        out_specs=qrow(width, 0),
        out_shape=jax.ShapeDtypeStruct((B, S, width), BF16),
        scratch_shapes=[pltpu.VMEM((S // tk, tk, tq), I32), pltpu.VMEM((S // tk, tk, tq), F32),
                        pltpu.VMEM((n_heads, tq, 2 * DSA_HEAD_DIM), BF16),
                        pltpu.VMEM((width, tq), F32)],
        compiler_params=_cparams(2), name="dsa_attn")(main, main, vt, iq, ik, iwt)


def _outproj_kernel(a_ref, b_ref, h_ref, w_ref, g_ref, beta_ref, o_ref, *, alpha):
    wa = a_ref.shape[1]
    y = jnp.dot(a_ref[...], w_ref[:wa, :], preferred_element_type=F32)
    y = y + jnp.dot(b_ref[...], w_ref[wa:, :], preferred_element_type=F32)
    o_ref[...] = _layer_norm(alpha * h_ref[...] + y, g_ref[...], beta_ref[...])


def _output_projection(a, b, h, w_out, g, beta, alpha, tm=512):
    T, D = h.shape
    rows = lambda n: pl.BlockSpec((tm, n), lambda t: (t, 0))
    vec = pl.BlockSpec((1, D), lambda t: (0, 0))
    return pl.pallas_call(
        functools.partial(_outproj_kernel, alpha=alpha), grid=(T // tm,),
        in_specs=[rows(a.shape[1]), rows(b.shape[1]), rows(D),
                  pl.BlockSpec(w_out.shape, lambda t: (0, 0)), vec, vec],
        out_specs=rows(D), out_shape=jax.ShapeDtypeStruct((T, D), F32),
        compiler_params=_cparams(1), name="out_proj_ln1")(
            a, b, h, w_out, g.reshape(1, D), beta.reshape(1, D))


def _top_rows(x, n):
    rows = []
    rank = jnp.full(x.shape, float(n), F32)
    for b in range(n):
        mx = jnp.max(x, axis=0, keepdims=True)
        rows.append(mx)
        hit = x == mx
        rank = jnp.where(hit, float(b), rank)
        x = jnp.where(hit, -jnp.inf, x)
    return rows, rank


def _route_kernel(h_ref, wq_ref, sk_ref, n1_ref, c1_ref, r2_ref, e2_ref):
    hb = h_ref[...].astype(BF16)
    qd = 2 * PEER_HALF_DIM
    k = PEER_TOPK
    for hd in range(PEER_HEADS):
        q = jnp.dot(hb, wq_ref[:, hd * qd:(hd + 1) * qd], preferred_element_type=F32).astype(BF16)
        s1 = _dot_nt(sk_ref[hd, 0], q[:, :PEER_HALF_DIM])
        s2 = _dot_nt(sk_ref[hd, 1], q[:, PEER_HALF_DIM:])
        top1, _ = _top_rows(s1, k)
        top2, rank2 = _top_rows(s2, k)
        top2 = jnp.concatenate(top2, axis=0)
        half = k // 2
        cand = [top1[0] + top2] + [top1[a] + top2[:half] for a in range(1, half)]
        cand.append(jnp.concatenate(top1[half:], axis=0) + top2[0:1])
        best, _ = _top_rows(jnp.concatenate(cand, axis=0), k + 1)
        z = jnp.ones_like(best[0])
        for r in best[1:k]:
            z = z + jnp.exp(r - best[0])
        cut = 0.5 * (best[k - 1] + best[k])
        n1 = jnp.zeros_like(s1)
        for b in range(k):
            n1 = n1 + jnp.where(s1 >= cut - top2[b:b + 1], 1.0, 0.0)
        n1_ref[hd] = n1
        c1_ref[hd] = jnp.where(s1 >= top1[-1], jnp.exp(s1 - top1[0]) / z, 0.0)
        r2_ref[hd] = rank2.astype(BF16)
        e2_ref[hd] = jnp.where(s2 >= top2[k - 1:], jnp.exp(s2 - top2[0:1]), 0.0).astype(BF16)


def _peer_route(h, wq, subkeys, tm=256):
    T, D = h.shape
    shape = (PEER_HEADS, PEER_N_KEYS, T)
    ospec = pl.BlockSpec((PEER_HEADS, PEER_N_KEYS, tm), lambda t: (0, 0, t))
    return pl.pallas_call(
        _route_kernel, grid=(T // tm,),
        in_specs=[pl.BlockSpec((tm, D), lambda t: (t, 0)),
                  pl.BlockSpec(wq.shape, lambda t: (0, 0)),
                  pl.BlockSpec(subkeys.shape, lambda t: (0, 0, 0, 0))],
        out_specs=[ospec] * 4,
        out_shape=[jax.ShapeDtypeStruct(shape, F32), jax.ShapeDtypeStruct(shape, F32),
                   jax.ShapeDtypeStruct(shape, BF16), jax.ShapeDtypeStruct(shape, BF16)],
        compiler_params=_cparams(1), name="peer_route")(h, wq, subkeys)


def _gelu(x):
    return 0.5 * x * (1.0 + lax.erf(x * (2.0 ** -0.5)))


PEER_TILE = 1024
MXU_DIM = 256
GATE_ROWS = 16


def _peer_kernel(h_ref, u_ref, vt_ref, n1_ref, c1_ref, r2_ref, e2_ref, o_ref,
                 xb_sc, acc_sc, part_sc, act_a, act_b, w_a, w_b, r2_sc, e2_sc, *, te, n_tiles):
    s = pl.program_id(1)
    tm, d = h_ref.shape

    @pl.when(s == 0)
    def _():
        xb_sc[...] = h_ref[...].astype(BF16)
        acc_sc[...] = jnp.zeros_like(acc_sc)
        r2_sc[...] = r2_ref[...]
        e2_sc[...] = e2_ref[...]

    per = te // PEER_N_KEYS
    n_k = d // MXU_DIM

    def step(act_new, act_old, w_new, w_old, stages):
        tile = s - 1

        def gate_piece(ii, lt):
            i = tile * per + ii
            lanes = slice(lt * LANES, (lt + 1) * LANES)
            nb = [jnp.broadcast_to(n1_ref[hd, pl.ds(i, 1), :][:, lanes].astype(BF16), (GATE_ROWS, LANES))
                  for hd in range(PEER_HEADS)]
            cb = [jnp.broadcast_to(c1_ref[hd, pl.ds(i, 1), :][:, lanes].astype(BF16), (GATE_ROWS, LANES))
                  for hd in range(PEER_HEADS)]
            zero = jnp.zeros((GATE_ROWS, LANES), BF16)
            for r0 in range(0, PEER_N_KEYS, GATE_ROWS):
                rows = slice(r0, r0 + GATE_ROWS)
                g = zero
                for hd in range(PEER_HEADS):
                    g = g + jnp.where(r2_sc[hd, rows, lanes] < nb[hd], e2_sc[hd, rows, lanes] * cb[hd], zero)
                arow = slice(ii * PEER_N_KEYS + r0, ii * PEER_N_KEYS + r0 + GATE_ROWS)
                w_new[arow, lanes] = g * act_old[arow, lanes]

        def stage1_piece(nc, kc):
            cols = slice(nc * MXU_DIM, (nc + 1) * MXU_DIM)
            kk = slice(kc * MXU_DIM, (kc + 1) * MXU_DIM)
            p = _dot_nt(u_ref[:, kk], xb_sc[cols, kk])
            if kc == 0:
                part_sc[:, cols] = p
            elif kc < n_k - 1:
                part_sc[:, cols] += p
            else:
                act_new[:, cols] = _gelu(part_sc[:, cols] + p).astype(BF16)

        def stage3_piece(nc, ec, mc):
            cols = slice(nc * MXU_DIM, (nc + 1) * MXU_DIM)
            ee = slice(ec * MXU_DIM, (ec + 1) * MXU_DIM)
            rr = slice(mc * (d // 2), (mc + 1) * (d // 2))
            acc_sc[rr, cols] += jnp.dot(vt_ref[rr, ee], w_old[ee, cols], preferred_element_type=F32)

        gates, mxu = [], []
        if 2 in stages:
            gates = [functools.partial(gate_piece, ii, lt) for ii in range(per) for lt in range(tm // LANES)]
        if 1 in stages:
            mxu += [functools.partial(stage1_piece, nc, kc) for nc in range(tm // MXU_DIM) for kc in range(n_k)]
        if 3 in stages:
            mxu += [functools.partial(stage3_piece, nc, ec, mc) for nc in range(tm // MXU_DIM)
                    for ec in range(te // MXU_DIM) for mc in range(2)]
        for b in range(max(len(gates), len(mxu))):
            if b < len(mxu):
                mxu[b]()
            if b < len(gates):
                gates[b]()

    act, w = (act_a, act_b), (w_a, w_b)
    steady = jnp.logical_and(s >= 2, s < n_tiles)
    variants = [(s == 0, 0, (1,)), (s == 1, 1, (1, 2)),
                (jnp.logical_and(steady, s % 2 == 0), 0, (1, 2, 3)),
                (jnp.logical_and(steady, s % 2 == 1), 1, (1, 2, 3)),
                (s == n_tiles, n_tiles % 2, (2, 3)), (s == n_tiles + 1, (n_tiles + 1) % 2, (3,))]
    for cond, par, stages in variants:
        @pl.when(cond)
        def _(par=par, stages=stages):
            step(act[par], act[1 - par], w[1 - par], w[par], stages)

    @pl.when(s == n_tiles + 1)
    def _():
        o_ref[...] = acc_sc[...].T


def _peer_experts(h, u_bf, vt_tiles, n1, c1, r2, e2, tm=512):
    T, D = h.shape
    n_tiles, _, te = vt_tiles.shape
    rspec = pl.BlockSpec((PEER_HEADS, PEER_N_KEYS, tm), lambda t, s: (0, 0, t))
    return pl.pallas_call(
        functools.partial(_peer_kernel, te=te, n_tiles=n_tiles), grid=(T // tm, n_tiles + 2),
        in_specs=[pl.BlockSpec((tm, D), lambda t, s: (t, 0)),
                  pl.BlockSpec((te, D), lambda t, s: (jnp.minimum(s, n_tiles - 1), 0)),
                  pl.BlockSpec((None, D, te), lambda t, s: (jnp.clip(s - 2, 0, n_tiles - 1), 0, 0)),
                  rspec, rspec, rspec, rspec],
        out_specs=pl.BlockSpec((tm, D), lambda t, s: (t, 0)),
        out_shape=jax.ShapeDtypeStruct((T, D), F32),
        scratch_shapes=[pltpu.VMEM((tm, D), BF16), pltpu.VMEM((D, tm), F32), pltpu.VMEM((te, tm), F32),
                        pltpu.VMEM((te, tm), BF16), pltpu.VMEM((te, tm), BF16),
                        pltpu.VMEM((te, tm), BF16), pltpu.VMEM((te, tm), BF16),
                        pltpu.VMEM((PEER_HEADS, PEER_N_KEYS, tm), BF16), pltpu.VMEM((PEER_HEADS, PEER_N_KEYS, tm), BF16)],
        compiler_params=_cparams(2), name="peer_experts")(h, u_bf, vt_tiles, n1, c1, r2, e2)


def _ple_kernel(h_ref, f_ref, p_ref, wg_ref, bg_ref, wp_ref, g_ref, beta_ref, o_ref, *, alpha):
    r = alpha * h_ref[...] + f_ref[...]
    z = jnp.dot(r.astype(BF16), wg_ref[...], preferred_element_type=F32) + bg_ref[...]
    gate = 1.0 / (1.0 + jnp.exp(-z))
    r = r + gate * jnp.dot(p_ref[...].astype(BF16), wp_ref[...], preferred_element_type=F32)
    o_ref[...] = _layer_norm(r, g_ref[...], beta_ref[...])


def _ple_ln2(h, f, p, layer, wg, bg, wp, g, beta, alpha, tm=512):
    T, D = h.shape
    pd = p.shape[-1]
    rows = pl.BlockSpec((tm, D), lambda t: (t, 0))
    vec = pl.BlockSpec((1, D), lambda t: (0, 0))
    return pl.pallas_call(
        functools.partial(_ple_kernel, alpha=alpha), grid=(T // tm,),
        in_specs=[rows, rows, pl.BlockSpec((None, tm, pd), lambda t: (layer, t, 0)),
                  pl.BlockSpec(wg.shape, lambda t: (0, 0)), vec,
                  pl.BlockSpec(wp.shape, lambda t: (0, 0)), vec, vec],
        out_specs=rows, out_shape=jax.ShapeDtypeStruct((T, D), F32),
        compiler_params=_cparams(1), name="ple_ln2")(
            h, f, p, wg, bg.reshape(1, D), wp, g.reshape(1, D), beta.reshape(1, D))


def _mixer_layer(h, B, S, w_in, w_out, lam_vecs, subln_g, lam_init, ln_g, ln_b, alpha):
    T, D = h.shape
    dw = D // 2
    n_diff = dw // (2 * DIFF_HEAD_DIM)
    sw = D - dw
    n_dsa = sw // DSA_HEAD_DIM
    n_iq = IDX_HEADS * IDX_DIM
    wb = w_in.astype(BF16)
    c_sv = 3 * dw + 2 * sw
    c_iq = c_sv + sw
    w_main = wb[:, :c_sv]
    w_vt = wb[:, c_sv:c_iq].T
    w_iq = wb[:, c_iq:c_iq + n_iq]
    w_ik = wb[:, c_iq + n_iq:c_iq + n_iq + IDX_DIM]
    w_iwt = jnp.pad(wb[:, c_iq + n_iq + IDX_DIM:].T, ((0, 8 - IDX_HEADS), (0, 0)))
    main, vt, iq, ik, iwt = _input_projection(h, w_main, w_vt, w_iq, w_ik, w_iwt)
    main = main.reshape(B, S, c_sv)
    a = _diff_attention(main, lam_vecs, subln_g, lam_init, n_diff, 0)
    b = _dsa_attention(main, vt, iq.reshape(B, S, -1), ik.reshape(B, S, -1), iwt, n_dsa,
                       3 * dw // sw, 3 * dw // sw + 1, 0)
    return _output_projection(a.reshape(T, -1), b.reshape(T, -1), h, w_out.astype(BF16), ln_g, ln_b, alpha)


def kernel(x, p, ln_in_g, ln_in_b, w_in, w_out, diff_lambda, diff_subln_g, ln1_g, ln1_b, peer_wq, peer_subkeys, peer_u, peer_v, ple_wg, ple_bg, ple_wp, ln2_g, ln2_b):
    B, S, D = x.shape
    depth = w_in.shape[0]
    T = B * S
    alpha = (2 * depth) ** 0.25
    h = _entry_layer_norm(x.reshape(T, D), ln_in_g, ln_in_b)
    p2 = p.reshape(depth, T, p.shape[-1])
    for i in range(depth):
        lam_init = 0.8 - 0.6 * math.exp(-0.3 * i)
        h = _mixer_layer(h, B, S, w_in[i], w_out[i], diff_lambda[i], diff_subln_g[i], lam_init,
                         ln1_g[i], ln1_b[i], alpha)
        n1, c1, r2, e2 = _peer_route(h, peer_wq[i].astype(BF16), peer_subkeys[i].astype(BF16))
        vt_tiles = peer_v[i].astype(BF16).reshape(-1, PEER_TILE, D).transpose(0, 2, 1)
        f = _peer_experts(h, peer_u[i].astype(BF16), vt_tiles, n1, c1, r2, e2)
        h = _ple_ln2(h, f, p2, i, ple_wg[i].astype(BF16), ple_bg[i], ple_wp[i].astype(BF16),
                     ln2_g[i], ln2_b[i], alpha)
    return h.reshape(B, S, D)
```

```python
import functools
import math
import struct

import jax
import jax.numpy as jnp
from jax import lax
from jax.experimental import pallas as pl
from jax.experimental.pallas import tpu as pltpu

F32 = jnp.float32
BF16 = jnp.bfloat16
I32 = jnp.int32

LN_EPS = 1e-5
NEG_INF = -1e30
CHUNK = 64
CHUNK_SHIFT = 6
LANES = 128
VMEM_LIMIT = 56 * 1024 * 1024

DIFF_HEAD_DIM = 64
DSA_HEAD_DIM = 64
IDX_HEADS = 4
IDX_DIM = 64
IDX_TOPK_MAX = 256
PEER_HEADS = 8
PEER_N_KEYS = 128
PEER_HALF_DIM = 128
PEER_TOPK = 16

_NEG_BITS = struct.unpack("<i", struct.pack("<f", NEG_INF))[0]
KEY_NEG = _NEG_BITS ^ 0x7FFFFFFF


def _cparams(n_axes):
    return pltpu.CompilerParams(
        dimension_semantics=("arbitrary",) * n_axes, vmem_limit_bytes=VMEM_LIMIT)


def _dot_nt(a, b):
    return lax.dot_general(a, b, (((1,), (1,)), ((), ())), preferred_element_type=F32)


def _layer_norm(x, g, b):
    mu = jnp.mean(x, axis=-1, keepdims=True)
    xc = x - mu
    var = jnp.mean(xc * xc, axis=-1, keepdims=True)
    return xc * lax.rsqrt(var + LN_EPS) * g + b


def _ln_kernel(x_ref, g_ref, b_ref, o_ref):
    o_ref[...] = _layer_norm(x_ref[...], g_ref[...], b_ref[...])


def _entry_layer_norm(x, g, b, tm=512):
    T, D = x.shape
    row = pl.BlockSpec((tm, D), lambda t: (t, 0))
    vec = pl.BlockSpec((1, D), lambda t: (0, 0))
    return pl.pallas_call(
        _ln_kernel, grid=(T // tm,), in_specs=[row, vec, vec], out_specs=row,
        out_shape=jax.ShapeDtypeStruct((T, D), F32), compiler_params=_cparams(1),
        name="entry_ln")(x, g.reshape(1, D), b.reshape(1, D))


def _inproj_kernel(h_ref, wm_ref, wvt_ref, wiq_ref, wik_ref, wiwt_ref, main_ref, vt_ref, iq_ref, ik_ref, iwt_ref, *, tn):
    hb = h_ref[...].astype(BF16)
    for j in range(0, wm_ref.shape[1], tn):
        main_ref[:, j:j + tn] = jnp.dot(hb, wm_ref[:, j:j + tn], preferred_element_type=F32).astype(BF16)
    vt_ref[0] = _dot_nt(wvt_ref[...], hb).astype(BF16)
    iq_ref[...] = jnp.dot(hb, wiq_ref[...], preferred_element_type=F32) * (IDX_DIM ** -0.5)
    ik_ref[...] = jnp.dot(hb, wik_ref[...], preferred_element_type=F32)
    iwt_ref[...] = _dot_nt(wiwt_ref[...], hb) * (IDX_HEADS ** -0.5)


def _input_projection(h, w_main, w_vt, w_iq, w_ik, w_iwt, tm=512, tn=512):
    T, D = h.shape
    nm = w_main.shape[1]
    nv = w_vt.shape[0]

    def full(w):
        return pl.BlockSpec(w.shape, lambda t: (0, 0))

    def rows(n):
        return pl.BlockSpec((tm, n), lambda t: (t, 0))

    return pl.pallas_call(
        functools.partial(_inproj_kernel, tn=tn), grid=(T // tm,),
        in_specs=[rows(D), full(w_main), full(w_vt), full(w_iq), full(w_ik), full(w_iwt)],
        out_specs=[rows(nm), pl.BlockSpec((1, nv, tm), lambda t: (t, 0, 0)),
                   rows(w_iq.shape[1]), rows(w_ik.shape[1]),
                   pl.BlockSpec((w_iwt.shape[0], tm), lambda t: (0, t))],
        out_shape=[jax.ShapeDtypeStruct((T, nm), BF16),
                   jax.ShapeDtypeStruct((T // tm, nv, tm), BF16),
                   jax.ShapeDtypeStruct((T, w_iq.shape[1]), F32),
                   jax.ShapeDtypeStruct((T, w_ik.shape[1]), F32),
                   jax.ShapeDtypeStruct((w_iwt.shape[0], T), F32)],
        compiler_params=_cparams(1), name="in_proj")(h, w_main, w_vt, w_iq, w_ik, w_iwt)


def _diff_kernel(lamv_ref, g_ref, q_ref, k_ref, v_ref, o_ref, qm_sc, acc_sc, *, tq, tk, lam_init, slopes):
    qi = pl.program_id(1)
    dv = 2 * DIFF_HEAD_DIM
    n_heads = len(slopes)
    lane = lax.broadcasted_iota(I32, (tq, dv), 1)
    for hd in range(n_heads):
        qs = q_ref[0, :, hd * dv:(hd + 1) * dv] * (DIFF_HEAD_DIM ** -0.5)
        zero = jnp.zeros_like(qs)
        qm_sc[2 * hd] = jnp.where(lane < DIFF_HEAD_DIM, qs, zero)
        qm_sc[2 * hd + 1] = jnp.where(lane >= DIFF_HEAD_DIM, qs, zero)
    acc_sc[...] = jnp.zeros_like(acc_sc)
    t_idx = qi * tq + lax.broadcasted_iota(I32, (tq, tk), 0)

    def step(kb, carry, diagonal):
        off = pl.multiple_of(kb * tk, tk)
        s_idx = off + lax.broadcasted_iota(I32, (tq, tk), 1)
        dist = jnp.abs(t_idx - s_idx).astype(F32)
        if diagonal:
            visible = s_idx < (((t_idx >> CHUNK_SHIFT) + 1) << CHUNK_SHIFT)

        def qk(c):
            return _dot_nt(qm_sc[c], k_ref[0, pl.ds(off, tk), (c // 2) * dv:(c // 2 + 1) * dv])

        def pv(c, a, p):
            v = v_ref[0, pl.ds(off, tk), (c // 2) * dv:(c // 2 + 1) * dv]
            acc_sc[c] = a * acc_sc[c] + jnp.dot(p, v, preferred_element_type=F32)

        out = []
        s_next = qk(0)
        pending = None
        for c in range(2 * n_heads):
            s = s_next - slopes[c // 2] * dist
            if c + 1 < 2 * n_heads:
                s_next = qk(c + 1)
            if diagonal:
                s = jnp.where(visible, s, NEG_INF)
            m, l = carry[2 * c], carry[2 * c + 1]
            m_new = jnp.maximum(m, jnp.max(s, axis=-1, keepdims=True))
            p = jnp.exp(s - m_new)
            a = jnp.exp(m - m_new)
            out.extend((m_new, a * l + jnp.sum(p, axis=-1, keepdims=True)))
            if pending is not None:
                pv(*pending)
            pending = (c, a, p.astype(BF16))
        pv(*pending)
        return tuple(out)

    m0 = jnp.full((tq, 1), -jnp.inf, F32)
    l0 = jnp.zeros((tq, 1), F32)
    n_full = (qi * tq) // tk
    carry = lax.fori_loop(0, n_full, functools.partial(step, diagonal=False), (m0, l0) * (2 * n_heads))
    for d in range(max(tq // tk, 1)):
        carry = step(n_full + d, carry, True)

    lv = lamv_ref[...]
    lam = (jnp.exp(jnp.sum(lv[0:1] * lv[1:2], axis=-1, keepdims=True))
           - jnp.exp(jnp.sum(lv[2:3] * lv[3:4], axis=-1, keepdims=True)) + lam_init)
    for hd in range(n_heads):
        o = acc_sc[2 * hd] / carry[4 * hd + 1] - lam * (acc_sc[2 * hd + 1] / carry[4 * hd + 3])
        o = o * lax.rsqrt(jnp.mean(o * o, axis=-1, keepdims=True) + LN_EPS) * g_ref[...]
        o_ref[0, :, hd * dv:(hd + 1) * dv] = (o * (1.0 - lam_init)).astype(o_ref.dtype)


def _diff_attention(main, lam_vecs, subln_g, lam_init, n_heads, col0, tq=256, tk=512):
    B, S, _ = main.shape
    dv = 2 * DIFF_HEAD_DIM
    width = n_heads * dv
    slopes = tuple(2.0 ** (-8.0 * (i + 1) / n_heads) for i in range(n_heads))
    kv = lambda c: pl.BlockSpec((1, S, width), lambda b, q: (b, 0, c))
    qo = lambda c: pl.BlockSpec((1, tq, width), lambda b, q: (b, q, c))
    return pl.pallas_call(
        functools.partial(_diff_kernel, tq=tq, tk=tk, lam_init=lam_init, slopes=slopes),
        grid=(B, S // tq),
        in_specs=[pl.BlockSpec(lam_vecs.shape, lambda b, q: (0, 0)),
                  pl.BlockSpec((1, dv), lambda b, q: (0, 0)),
                  qo(col0), kv(col0 + 1), kv(col0 + 2)],
        out_specs=qo(0),
        out_shape=jax.ShapeDtypeStruct((B, S, width), BF16),
        scratch_shapes=[pltpu.VMEM((2 * n_heads, tq, dv), BF16), pltpu.VMEM((2 * n_heads, tq, dv), F32)],
        compiler_params=_cparams(2), name="diff_attn")(
            lam_vecs, subln_g.reshape(1, dv), main, main, main)


def _sort_key(x):
    bits = pltpu.bitcast(x + 0.0, I32)
    return jnp.where(bits < 0, bits ^ 0x7FFFFFFF, bits)


def _dsa_kernel(q_ref, k_ref, vt_ref, iq_ref, ik_ref, iwt_ref, o_ref, key_sc, mask_sc, qm_sc, acc_sc, *,
                tq, tk, seq, topk, slopes):
    qi = pl.program_id(1)
    q_pos0 = qi * tq
    nkb = (q_pos0 + tq + tk - 1) // tk
    n_tail = seq - nkb * tk
    t_row = q_pos0 + lax.broadcasted_iota(I32, (1, tq), 1)
    chunk_end = ((t_row >> CHUNK_SHIFT) + 1) << CHUNK_SHIFT
    s_col = lax.broadcasted_iota(I32, (tk, tq), 0)

    iq = iq_ref[0].astype(BF16)
    iq_heads = [iq[:, j * IDX_DIM:(j + 1) * IDX_DIM] for j in range(IDX_HEADS)]
    iw_rows = [iwt_ref[j:j + 1, :] for j in range(IDX_HEADS)]

    def score_block(kb, _):
        off = pl.multiple_of(kb * tk, tk)
        ik = ik_ref[0, pl.ds(off, tk), :].astype(BF16)
        acc = jnp.zeros((tk, tq), F32)
        for j in range(IDX_HEADS):
            acc = acc + iw_rows[j] * jnp.maximum(_dot_nt(ik, iq_heads[j]), 0.0)
        key_sc[kb] = _sort_key(jnp.where(off + s_col < chunk_end, acc, NEG_INF))
        return 0

    lax.fori_loop(0, nkb, score_block, 0)

    def count(pred):
        def body(kb, c):
            return c + jnp.sum(pred(key_sc[kb]).astype(I32), axis=0, keepdims=True)
        return lax.fori_loop(0, nkb, body, jnp.zeros((1, tq), I32))

    bits_per_check = 4

    def searching(st):
        i, _, done = st
        return jnp.logical_and(i < 32, jnp.min(done) == 0)

    def bit_steps(st):
        i, r, done = st
        for u in range(bits_per_check):
            cand = jnp.where(done > 0, r, r ^ (jnp.int32(1) << (31 - i - u)))
            cnt = count(lambda kk: kk >= cand) + jnp.where(cand <= KEY_NEG, n_tail, 0)
            r = jnp.where(cnt >= topk, cand, r)
            done = jnp.where(cnt == topk, 1, done)
        return i + bits_per_check, r, done

    _, tau, _ = lax.while_loop(
        searching, bit_steps, (jnp.int32(0), jnp.full((1, tq), -2 ** 31, I32), jnp.zeros((1, tq), I32)))
    cnt_gt = count(lambda kk: kk > tau) + jnp.where(tau < KEY_NEG, n_tail, 0)
    cnt_eq = count(lambda kk: kk == tau) + jnp.where(tau == KEY_NEG, n_tail, 0)
    need = topk - cnt_gt
    has_ties = jnp.max(cnt_eq - need) > 0

    @pl.when(jnp.logical_not(has_ties))
    def _():
        def body(kb, _):
            sel = jnp.logical_and(key_sc[kb] >= tau, kb * tk + s_col < chunk_end)
            mask_sc[kb] = jnp.where(sel, 0.0, NEG_INF)
            return 0
        lax.fori_loop(0, nkb, body, 0)

    @pl.when(has_ties)
    def _():
        r_i = lax.broadcasted_iota(I32, (tk, tk), 0)
        c_i = lax.broadcasted_iota(I32, (tk, tk), 1)
        earlier = jnp.where(c_i < r_i, 1.0, 0.0).astype(BF16)

        def body(kb, seen):
            kk = key_sc[kb]
            eq = kk == tau
            eqf = jnp.where(eq, 1.0, 0.0)
            rank = seen + jnp.dot(earlier, eqf.astype(BF16), preferred_element_type=F32)
            take = jnp.logical_and(eq, rank < need.astype(F32))
            sel = jnp.logical_and(jnp.logical_or(kk > tau, take), kb * tk + s_col < chunk_end)
            mask_sc[kb] = jnp.where(sel, 0.0, NEG_INF)
            return seen + jnp.sum(eqf, axis=0, keepdims=True)
        lax.fori_loop(0, nkb, body, jnp.zeros((1, tq), F32))

    hdim = DSA_HEAD_DIM
    pair = 2 * hdim
    n_heads = len(slopes)
    lane = lax.broadcasted_iota(I32, (tq, pair), 1)
    for p in range(n_heads // 2):
        qs = q_ref[0, :, p * pair:(p + 1) * pair] * (hdim ** -0.5)
        zero = jnp.zeros_like(qs)
        qm_sc[2 * p] = jnp.where(lane < hdim, qs, zero)
        qm_sc[2 * p + 1] = jnp.where(lane >= hdim, qs, zero)
    acc_sc[...] = jnp.zeros_like(acc_sc)

    def attend(kb, carry):
        off = pl.multiple_of(kb * tk, tk)
        dist = jnp.abs(t_row - (off + s_col)).astype(F32)
        madd = mask_sc[kb]

        def qk(hd):
            k = k_ref[0, pl.ds(off, tk), (hd // 2) * pair:(hd // 2 + 1) * pair]
            return _dot_nt(k, qm_sc[hd])

        def pv(hd, a, pr):
            rows = slice(hd * hdim, (hd + 1) * hdim)
            acc_sc[rows, :] = a * acc_sc[rows, :] + jnp.dot(vt_ref[kb, rows, :], pr, preferred_element_type=F32)

        out = []
        s_next = qk(0)
        pending = None
        for hd in range(n_heads):
            s = s_next - slopes[hd] * dist + madd
            if hd + 1 < n_heads:
                s_next = qk(hd + 1)
            m, l = carry[2 * hd], carry[2 * hd + 1]
            m_new = jnp.maximum(m, jnp.max(s, axis=0, keepdims=True))
            pr = jnp.exp(s - m_new)
            a = jnp.exp(m - m_new)
            out.extend((m_new, a * l + jnp.sum(pr, axis=0, keepdims=True)))
            if pending is not None:
                pv(*pending)
            pending = (hd, a, pr.astype(BF16))
        pv(*pending)
        return tuple(out)

    m0 = jnp.full((1, tq), -jnp.inf, F32)
    l0 = jnp.zeros((1, tq), F32)
    carry = lax.fori_loop(0, nkb, attend, (m0, l0) * n_heads)
    for hd in range(n_heads):
        rows = slice(hd * hdim, (hd + 1) * hdim)
        acc_sc[rows, :] = acc_sc[rows, :] / carry[2 * hd + 1]
    o_ref[0] = acc_sc[...].T.astype(o_ref.dtype)


def _dsa_attention(main, vt, iq, ik, iwt, n_heads, col_q, col_k, vrow0, tq=512):
    B, S, _ = main.shape
    tk = vt.shape[-1]
    width = n_heads * DSA_HEAD_DIM
    nq = S // tq
    topk = min(IDX_TOPK_MAX, S // 4)
    slopes = tuple(2.0 ** (-8.0 * (i + 1) / n_heads) for i in range(n_heads))
    qrow = lambda n, c: pl.BlockSpec((1, tq, n), lambda b, q: (b, q, c))
    seqb = lambda n, c: pl.BlockSpec((1, S, n), lambda b, q: (b, 0, c))
    return pl.pallas_call(
        functools.partial(_dsa_kernel, tq=tq, tk=tk, seq=S, topk=topk, slopes=slopes),
        grid=(B, nq),
        in_specs=[qrow(width, col_q), seqb(width, col_k),
                  pl.BlockSpec((S // tk, width, tk), lambda b, q: (b, vrow0, 0)),
                  qrow(iq.shape[-1], 0), seqb(ik.shape[-1], 0),
                  pl.BlockSpec((iwt.shape[0], tq), lambda b, q: (0, b * nq + q))],
        out_specs=qrow(width, 0),
        out_shape=jax.ShapeDtypeStruct((B, S, width), BF16),
        scratch_shapes=[pltpu.VMEM((S // tk, tk, tq), I32), pltpu.VMEM((S // tk, tk, tq), F32),
                        pltpu.VMEM((n_heads, tq, 2 * DSA_HEAD_DIM), BF16),
                        pltpu.VMEM((width, tq), F32)],
        compiler_params=_cparams(2), name="dsa_attn")(main, main, vt, iq, ik, iwt)


def _outproj_kernel(a_ref, b_ref, h_ref, w_ref, g_ref, beta_ref, o_ref, *, alpha):
    wa = a_ref.shape[1]
    y = jnp.dot(a_ref[...], w_ref[:wa, :], preferred_element_type=F32)
    y = y + jnp.dot(b_ref[...], w_ref[wa:, :], preferred_element_type=F32)
    o_ref[...] = _layer_norm(alpha * h_ref[...] + y, g_ref[...], beta_ref[...])


def _output_projection(a, b, h, w_out, g, beta, alpha, tm=512):
    T, D = h.shape
    rows = lambda n: pl.BlockSpec((tm, n), lambda t: (t, 0))
    vec = pl.BlockSpec((1, D), lambda t: (0, 0))
    return pl.pallas_call(
        functools.partial(_outproj_kernel, alpha=alpha), grid=(T // tm,),
        in_specs=[rows(a.shape[1]), rows(b.shape[1]), rows(D),
                  pl.BlockSpec(w_out.shape, lambda t: (0, 0)), vec, vec],
        out_specs=rows(D), out_shape=jax.ShapeDtypeStruct((T, D), F32),
        compiler_params=_cparams(1), name="out_proj_ln1")(
            a, b, h, w_out, g.reshape(1, D), beta.reshape(1, D))


def _top_rows(x, n):
    rows = []
    rank = jnp.full(x.shape, float(n), F32)
    for b in range(n):
        mx = jnp.max(x, axis=0, keepdims=True)
        rows.append(mx)
        hit = x == mx
        rank = jnp.where(hit, float(b), rank)
        x = jnp.where(hit, -jnp.inf, x)
    return rows, rank


def _route_kernel(h_ref, wq_ref, sk_ref, n1_ref, c1_ref, r2_ref, e2_ref):
    hb = h_ref[...].astype(BF16)
    qd = 2 * PEER_HALF_DIM
    k = PEER_TOPK
    for hd in range(PEER_HEADS):
        q = jnp.dot(hb, wq_ref[:, hd * qd:(hd + 1) * qd], preferred_element_type=F32).astype(BF16)
        s1 = _dot_nt(sk_ref[hd, 0], q[:, :PEER_HALF_DIM])
        s2 = _dot_nt(sk_ref[hd, 1], q[:, PEER_HALF_DIM:])
        top1, _ = _top_rows(s1, k)
        top2, rank2 = _top_rows(s2, k)
        top2 = jnp.concatenate(top2, axis=0)
        half = k // 2
        cand = [top1[0] + top2] + [top1[a] + top2[:half] for a in range(1, half)]
        cand.append(jnp.concatenate(top1[half:], axis=0) + top2[0:1])
        best, _ = _top_rows(jnp.concatenate(cand, axis=0), k + 1)
        z = jnp.ones_like(best[0])
        for r in best[1:k]:
            z = z + jnp.exp(r - best[0])
        cut = 0.5 * (best[k - 1] + best[k])
        n1 = jnp.zeros_like(s1)
        for b in range(k):
            n1 = n1 + jnp.where(s1 >= cut - top2[b:b + 1], 1.0, 0.0)
        n1_ref[hd] = n1
        c1_ref[hd] = jnp.where(s1 >= top1[-1], jnp.exp(s1 - top1[0]) / z, 0.0)
        r2_ref[hd] = rank2.astype(BF16)
        e2_ref[hd] = jnp.where(s2 >= top2[k - 1:], jnp.exp(s2 - top2[0:1]), 0.0).astype(BF16)


def _peer_route(h, wq, subkeys, tm=256):
    T, D = h.shape
    shape = (PEER_HEADS, PEER_N_KEYS, T)
    ospec = pl.BlockSpec((PEER_HEADS, PEER_N_KEYS, tm), lambda t: (0, 0, t))
    return pl.pallas_call(
        _route_kernel, grid=(T // tm,),
        in_specs=[pl.BlockSpec((tm, D), lambda t: (t, 0)),
                  pl.BlockSpec(wq.shape, lambda t: (0, 0)),
                  pl.BlockSpec(subkeys.shape, lambda t: (0, 0, 0, 0))],
        out_specs=[ospec] * 4,
        out_shape=[jax.ShapeDtypeStruct(shape, F32), jax.ShapeDtypeStruct(shape, F32),
                   jax.ShapeDtypeStruct(shape, BF16), jax.ShapeDtypeStruct(shape, BF16)],
        compiler_params=_cparams(1), name="peer_route")(h, wq, subkeys)


def _gelu(x):
    return 0.5 * x * (1.0 + lax.erf(x * (2.0 ** -0.5)))


PEER_TILE = 1024
MXU_DIM = 256
GATE_ROWS = 16


def _peer_kernel(h_ref, u_ref, vt_ref, n1_ref, c1_ref, r2_ref, e2_ref, o_ref,
                 xb_sc, acc_sc, part_sc, act_a, act_b, w_a, w_b, r2_sc, e2_sc, *, te, n_tiles):
    s = pl.program_id(1)
    tm, d = h_ref.shape

    @pl.when(s == 0)
    def _():
        xb_sc[...] = h_ref[...].astype(BF16)
        acc_sc[...] = jnp.zeros_like(acc_sc)
        r2_sc[...] = r2_ref[...]
        e2_sc[...] = e2_ref[...]

    per = te // PEER_N_KEYS
    n_k = d // MXU_DIM

    def step(act_new, act_old, w_new, w_old, stages):
        tile = s - 1

        def gate_piece(ii, lt):
            i = tile * per + ii
            lanes = slice(lt * LANES, (lt + 1) * LANES)
            nb = [jnp.broadcast_to(n1_ref[hd, pl.ds(i, 1), :][:, lanes].astype(BF16), (GATE_ROWS, LANES))
                  for hd in range(PEER_HEADS)]
            cb = [jnp.broadcast_to(c1_ref[hd, pl.ds(i, 1), :][:, lanes].astype(BF16), (GATE_ROWS, LANES))
                  for hd in range(PEER_HEADS)]
            zero = jnp.zeros((GATE_ROWS, LANES), BF16)
            for r0 in range(0, PEER_N_KEYS, GATE_ROWS):
                rows = slice(r0, r0 + GATE_ROWS)
                g = zero
                for hd in range(PEER_HEADS):
                    g = g + jnp.where(r2_sc[hd, rows, lanes] < nb[hd], e2_sc[hd, rows, lanes] * cb[hd], zero)
                arow = slice(ii * PEER_N_KEYS + r0, ii * PEER_N_KEYS + r0 + GATE_ROWS)
                w_new[arow, lanes] = g * act_old[arow, lanes]

        def stage1_piece(nc, kc):
            cols = slice(nc * MXU_DIM, (nc + 1) * MXU_DIM)
            kk = slice(kc * MXU_DIM, (kc + 1) * MXU_DIM)
            p = _dot_nt(u_ref[:, kk], xb_sc[cols, kk])
            if kc == 0:
                part_sc[:, cols] = p
            elif kc < n_k - 1:
                part_sc[:, cols] += p
            else:
                act_new[:, cols] = _gelu(part_sc[:, cols] + p).astype(BF16)

        def stage3_piece(nc, ec, mc):
            cols = slice(nc * MXU_DIM, (nc + 1) * MXU_DIM)
            ee = slice(ec * MXU_DIM, (ec + 1) * MXU_DIM)
            rr = slice(mc * (d // 2), (mc + 1) * (d // 2))
            acc_sc[rr, cols] += jnp.dot(vt_ref[rr, ee], w_old[ee, cols], preferred_element_type=F32)

        gates, mxu = [], []
        if 2 in stages:
            gates = [functools.partial(gate_piece, ii, lt) for ii in range(per) for lt in range(tm // LANES)]
        if 1 in stages:
            mxu += [functools.partial(stage1_piece, nc, kc) for nc in range(tm // MXU_DIM) for kc in range(n_k)]
        if 3 in stages:
            mxu += [functools.partial(stage3_piece, nc, ec, mc) for nc in range(tm // MXU_DIM)
                    for ec in range(te // MXU_DIM) for mc in range(2)]
        for b in range(max(len(gates), len(mxu))):
            if b < len(mxu):
                mxu[b]()
            if b < len(gates):
                gates[b]()

    act, w = (act_a, act_b), (w_a, w_b)
    steady = jnp.logical_and(s >= 2, s < n_tiles)
    variants = [(s == 0, 0, (1,)), (s == 1, 1, (1, 2)),
                (jnp.logical_and(steady, s % 2 == 0), 0, (1, 2, 3)),
                (jnp.logical_and(steady, s % 2 == 1), 1, (1, 2, 3)),
                (s == n_tiles, n_tiles % 2, (2, 3)), (s == n_tiles + 1, (n_tiles + 1) % 2, (3,))]
    for cond, par, stages in variants:
        @pl.when(cond)
        def _(par=par, stages=stages):
            step(act[par], act[1 - par], w[1 - par], w[par], stages)

    @pl.when(s == n_tiles + 1)
    def _():
        o_ref[...] = acc_sc[...].T


def _peer_experts(h, u_bf, vt_tiles, n1, c1, r2, e2, tm=512):
    T, D = h.shape
    n_tiles, _, te = vt_tiles.shape
    rspec = pl.BlockSpec((PEER_HEADS, PEER_N_KEYS, tm), lambda t, s: (0, 0, t))
    return pl.pallas_call(
        functools.partial(_peer_kernel, te=te, n_tiles=n_tiles), grid=(T // tm, n_tiles + 2),
        in_specs=[pl.BlockSpec((tm, D), lambda t, s: (t, 0)),
                  pl.BlockSpec((te, D), lambda t, s: (jnp.minimum(s, n_tiles - 1), 0)),
                  pl.BlockSpec((None, D, te), lambda t, s: (jnp.clip(s - 2, 0, n_tiles - 1), 0, 0)),
                  rspec, rspec, rspec, rspec],
        out_specs=pl.BlockSpec((tm, D), lambda t, s: (t, 0)),
        out_shape=jax.ShapeDtypeStruct((T, D), F32),
        scratch_shapes=[pltpu.VMEM((tm, D), BF16), pltpu.VMEM((D, tm), F32), pltpu.VMEM((te, tm), F32),
                        pltpu.VMEM((te, tm), BF16), pltpu.VMEM((te, tm), BF16),
                        pltpu.VMEM((te, tm), BF16), pltpu.VMEM((te, tm), BF16),
                        pltpu.VMEM((PEER_HEADS, PEER_N_KEYS, tm), BF16), pltpu.VMEM((PEER_HEADS, PEER_N_KEYS, tm), BF16)],
        compiler_params=_cparams(2), name="peer_experts")(h, u_bf, vt_tiles, n1, c1, r2, e2)


def _ple_kernel(h_ref, f_ref, p_ref, wg_ref, bg_ref, wp_ref, g_ref, beta_ref, o_ref, *, alpha):
    r = alpha * h_ref[...] + f_ref[...]
    z = jnp.dot(r.astype(BF16), wg_ref[...], preferred_element_type=F32) + bg_ref[...]
    gate = 1.0 / (1.0 + jnp.exp(-z))
    r = r + gate * jnp.dot(p_ref[...].astype(BF16), wp_ref[...], preferred_element_type=F32)
    o_ref[...] = _layer_norm(r, g_ref[...], beta_ref[...])


def _ple_ln2(h, f, p, layer, wg, bg, wp, g, beta, alpha, tm=512):
    T, D = h.shape
    pd = p.shape[-1]
    rows = pl.BlockSpec((tm, D), lambda t: (t, 0))
    vec = pl.BlockSpec((1, D), lambda t: (0, 0))
    return pl.pallas_call(
        functools.partial(_ple_kernel, alpha=alpha), grid=(T // tm,),
        in_specs=[rows, rows, pl.BlockSpec((None, tm, pd), lambda t: (layer, t, 0)),
                  pl.BlockSpec(wg.shape, lambda t: (0, 0)), vec,
                  pl.BlockSpec(wp.shape, lambda t: (0, 0)), vec, vec],
        out_specs=rows, out_shape=jax.ShapeDtypeStruct((T, D), F32),
        compiler_params=_cparams(1), name="ple_ln2")(
            h, f, p, wg, bg.reshape(1, D), wp, g.reshape(1, D), beta.reshape(1, D))


def _mixer_layer(h, B, S, w_in, w_out, lam_vecs, subln_g, lam_init, ln_g, ln_b, alpha):
    T, D = h.shape
    dw = D // 2
    n_diff = dw // (2 * DIFF_HEAD_DIM)
    sw = D - dw
    n_dsa = sw // DSA_HEAD_DIM
    n_iq = IDX_HEADS * IDX_DIM
    wb = w_in.astype(BF16)
    c_sv = 3 * dw + 2 * sw
    c_iq = c_sv + sw
    w_main = wb[:, :c_sv]
    w_vt = wb[:, c_sv:c_iq].T
    w_iq = wb[:, c_iq:c_iq + n_iq]
    w_ik = wb[:, c_iq + n_iq:c_iq + n_iq + IDX_DIM]
    w_iwt = jnp.pad(wb[:, c_iq + n_iq + IDX_DIM:].T, ((0, 8 - IDX_HEADS), (0, 0)))
    main, vt, iq, ik, iwt = _input_projection(h, w_main, w_vt, w_iq, w_ik, w_iwt)
    main = main.reshape(B, S, c_sv)
    a = _diff_attention(main, lam_vecs, subln_g, lam_init, n_diff, 0)
    b = _dsa_attention(main, vt, iq.reshape(B, S, -1), ik.reshape(B, S, -1), iwt, n_dsa,
                       3 * dw // sw, 3 * dw // sw + 1, 0)
    return _output_projection(a.reshape(T, -1), b.reshape(T, -1), h, w_out.astype(BF16), ln_g, ln_b, alpha)


def kernel(x, p, ln_in_g, ln_in_b, w_in, w_out, diff_lambda, diff_subln_g, ln1_g, ln1_b, peer_wq, peer_subkeys, peer_u, peer_v, ple_wg, ple_bg, ple_wp, ln2_g, ln2_b):
    B, S, D = x.shape
    depth = w_in.shape[0]
    T = B * S
    alpha = (2 * depth) ** 0.25
    h = _entry_layer_norm(x.reshape(T, D), ln_in_g, ln_in_b)
    p2 = p.reshape(depth, T, p.shape[-1])
    for i in range(depth):
        lam_init = 0.8 - 0.6 * math.exp(-0.3 * i)
        h = _mixer_layer(h, B, S, w_in[i], w_out[i], diff_lambda[i], diff_subln_g[i], lam_init,
                         ln1_g[i], ln1_b[i], alpha)
        n1, c1, r2, e2 = _peer_route(h, peer_wq[i].astype(BF16), peer_subkeys[i].astype(BF16))
        vt_tiles = peer_v[i].astype(BF16).reshape(-1, PEER_TILE, D).transpose(0, 2, 1)
        f = _peer_experts(h, peer_u[i].astype(BF16), vt_tiles, n1, c1, r2, e2)
        h = _ple_ln2(h, f, p2, i, ple_wg[i].astype(BF16), ple_bg[i], ple_wp[i].astype(BF16),
                     ln2_g[i], ln2_b[i], alpha)
    return h.reshape(B, S, D)
```

```python
import functools
import math
import struct

import jax
import jax.numpy as jnp
from jax import lax
from jax.experimental import pallas as pl
from jax.experimental.pallas import tpu as pltpu

F32 = jnp.float32
BF16 = jnp.bfloat16
I32 = jnp.int32

LN_EPS = 1e-5
NEG_INF = -1e30
CHUNK = 64
CHUNK_SHIFT = 6
LOG2E = math.log2(math.e)
LANES = 128
VMEM_LIMIT = 56 * 1024 * 1024

DIFF_HEAD_DIM = 64
DSA_HEAD_DIM = 64
IDX_HEADS = 4
IDX_DIM = 64
IDX_TOPK_MAX = 256
PEER_HEADS = 8
PEER_N_KEYS = 128
PEER_HALF_DIM = 128
PEER_TOPK = 16

_NEG_BITS = struct.unpack("<i", struct.pack("<f", NEG_INF))[0]
KEY_NEG = _NEG_BITS ^ 0x7FFFFFFF


def _cparams(n_axes):
    return pltpu.CompilerParams(
        dimension_semantics=("arbitrary",) * n_axes, vmem_limit_bytes=VMEM_LIMIT)


def _dot_nt(a, b):
    return lax.dot_general(a, b, (((1,), (1,)), ((), ())), preferred_element_type=F32)


def _layer_norm(x, g, b):
    mu = jnp.mean(x, axis=-1, keepdims=True)
    xc = x - mu
    var = jnp.mean(xc * xc, axis=-1, keepdims=True)
    return xc * lax.rsqrt(var + LN_EPS) * g + b


def _ln_kernel(x_ref, g_ref, b_ref, o_ref):
    o_ref[...] = _layer_norm(x_ref[...], g_ref[...], b_ref[...])


def _entry_layer_norm(x, g, b, tm=512):
    T, D = x.shape
    row = pl.BlockSpec((tm, D), lambda t: (t, 0))
    vec = pl.BlockSpec((1, D), lambda t: (0, 0))
    return pl.pallas_call(
        _ln_kernel, grid=(T // tm,), in_specs=[row, vec, vec], out_specs=row,
        out_shape=jax.ShapeDtypeStruct((T, D), F32), compiler_params=_cparams(1),
        name="entry_ln")(x, g.reshape(1, D), b.reshape(1, D))


def _inproj_kernel(h_ref, wm_ref, wvt_ref, wiq_ref, wik_ref, wiwt_ref, main_ref, vt_ref, iq_ref, ik_ref, iwt_ref, *, tn):
    hb = h_ref[...].astype(BF16)
    for j in range(0, wm_ref.shape[1], tn):
        main_ref[:, j:j + tn] = jnp.dot(hb, wm_ref[:, j:j + tn], preferred_element_type=F32).astype(BF16)
    vt_ref[0] = _dot_nt(wvt_ref[...], hb).astype(BF16)
    iq_ref[...] = jnp.dot(hb, wiq_ref[...], preferred_element_type=F32) * (IDX_DIM ** -0.5)
    ik_ref[...] = jnp.dot(hb, wik_ref[...], preferred_element_type=F32)
    iwt_ref[...] = _dot_nt(wiwt_ref[...], hb) * (IDX_HEADS ** -0.5)


def _input_projection(h, w_main, w_vt, w_iq, w_ik, w_iwt, tm=512, tn=512):
    T, D = h.shape
    nm = w_main.shape[1]
    nv = w_vt.shape[0]

    def full(w):
        return pl.BlockSpec(w.shape, lambda t: (0, 0))

    def rows(n):
        return pl.BlockSpec((tm, n), lambda t: (t, 0))

    return pl.pallas_call(
        functools.partial(_inproj_kernel, tn=tn), grid=(T // tm,),
        in_specs=[rows(D), full(w_main), full(w_vt), full(w_iq), full(w_ik), full(w_iwt)],
        out_specs=[rows(nm), pl.BlockSpec((1, nv, tm), lambda t: (t, 0, 0)),
                   rows(w_iq.shape[1]), rows(w_ik.shape[1]),
                   pl.BlockSpec((w_iwt.shape[0], tm), lambda t: (0, t))],
        out_shape=[jax.ShapeDtypeStruct((T, nm), BF16),
                   jax.ShapeDtypeStruct((T // tm, nv, tm), BF16),
                   jax.ShapeDtypeStruct((T, w_iq.shape[1]), F32),
                   jax.ShapeDtypeStruct((T, w_ik.shape[1]), F32),
                   jax.ShapeDtypeStruct((w_iwt.shape[0], T), F32)],
        compiler_params=_cparams(1), name="in_proj")(h, w_main, w_vt, w_iq, w_ik, w_iwt)


def _diff_kernel(lamv_ref, g_ref, q_ref, k_ref, v_ref, o_ref, qm_sc, acc_sc, *, tq, tk, lam_init, slopes):
    qi = pl.program_id(1)
    dv = 2 * DIFF_HEAD_DIM
    n_heads = len(slopes)
    lane = lax.broadcasted_iota(I32, (tq, dv), 1)
    for hd in range(n_heads):
        qs = q_ref[0, :, hd * dv:(hd + 1) * dv] * (DIFF_HEAD_DIM ** -0.5 * LOG2E)
        zero = jnp.zeros_like(qs)
        qm_sc[2 * hd] = jnp.where(lane < DIFF_HEAD_DIM, qs, zero)
        qm_sc[2 * hd + 1] = jnp.where(lane >= DIFF_HEAD_DIM, qs, zero)
    acc_sc[...] = jnp.zeros_like(acc_sc)
    t_idx = qi * tq + lax.broadcasted_iota(I32, (tq, tk), 0)

    def step(kb, carry, diagonal):
        off = pl.multiple_of(kb * tk, tk)
        s_idx = off + lax.broadcasted_iota(I32, (tq, tk), 1)
        dist = jnp.abs(t_idx - s_idx).astype(F32)
        if diagonal:
            visible = s_idx < (((t_idx >> CHUNK_SHIFT) + 1) << CHUNK_SHIFT)

        def qk(c):
            return _dot_nt(qm_sc[c], k_ref[0, pl.ds(off, tk), (c // 2) * dv:(c // 2 + 1) * dv])

        def pv(c, a, p):
            v = v_ref[0, pl.ds(off, tk), (c // 2) * dv:(c // 2 + 1) * dv]
            acc_sc[c] = a * acc_sc[c] + jnp.dot(p, v, preferred_element_type=F32)

        out = []
        s_next = qk(0)
        pending = None
        for c in range(2 * n_heads):
            s = s_next - (slopes[c // 2] * LOG2E) * dist
            if c + 1 < 2 * n_heads:
                s_next = qk(c + 1)
            if diagonal:
                s = jnp.where(visible, s, NEG_INF)
            m, l = carry[2 * c], carry[2 * c + 1]
            m_new = jnp.maximum(m, jnp.max(s, axis=-1, keepdims=True))
            p = jnp.exp2(s - m_new)
            a = jnp.exp2(m - m_new)
            out.extend((m_new, a * l + jnp.sum(p, axis=-1, keepdims=True)))
            if pending is not None:
                pv(*pending)
            pending = (c, a, p.astype(BF16))
        pv(*pending)
        return tuple(out)

    m0 = jnp.full((tq, 1), -jnp.inf, F32)
    l0 = jnp.zeros((tq, 1), F32)
    n_full = (qi * tq) // tk
    carry = lax.fori_loop(0, n_full, functools.partial(step, diagonal=False), (m0, l0) * (2 * n_heads))
    for d in range(max(tq // tk, 1)):
        carry = step(n_full + d, carry, True)

    lv = lamv_ref[...]
    lam = (jnp.exp(jnp.sum(lv[0:1] * lv[1:2], axis=-1, keepdims=True))
           - jnp.exp(jnp.sum(lv[2:3] * lv[3:4], axis=-1, keepdims=True)) + lam_init)
    for hd in range(n_heads):
        o = acc_sc[2 * hd] / carry[4 * hd + 1] - lam * (acc_sc[2 * hd + 1] / carry[4 * hd + 3])
        o = o * lax.rsqrt(jnp.mean(o * o, axis=-1, keepdims=True) + LN_EPS) * g_ref[...]
        o_ref[0, :, hd * dv:(hd + 1) * dv] = (o * (1.0 - lam_init)).astype(o_ref.dtype)


def _diff_attention(main, lam_vecs, subln_g, lam_init, n_heads, col0, tq=256, tk=512):
    B, S, _ = main.shape
    dv = 2 * DIFF_HEAD_DIM
    width = n_heads * dv
    slopes = tuple(2.0 ** (-8.0 * (i + 1) / n_heads) for i in range(n_heads))
    kv = lambda c: pl.BlockSpec((1, S, width), lambda b, q: (b, 0, c))
    qo = lambda c: pl.BlockSpec((1, tq, width), lambda b, q: (b, q, c))
    return pl.pallas_call(
        functools.partial(_diff_kernel, tq=tq, tk=tk, lam_init=lam_init, slopes=slopes),
        grid=(B, S // tq),
        in_specs=[pl.BlockSpec(lam_vecs.shape, lambda b, q: (0, 0)),
                  pl.BlockSpec((1, dv), lambda b, q: (0, 0)),
                  qo(col0), kv(col0 + 1), kv(col0 + 2)],
        out_specs=qo(0),
        out_shape=jax.ShapeDtypeStruct((B, S, width), BF16),
        scratch_shapes=[pltpu.VMEM((2 * n_heads, tq, dv), BF16), pltpu.VMEM((2 * n_heads, tq, dv), F32)],
        compiler_params=_cparams(2), name="diff_attn")(
            lam_vecs, subln_g.reshape(1, dv), main, main, main)


def _sort_key(x):
    bits = pltpu.bitcast(x + 0.0, I32)
    return jnp.where(bits < 0, bits ^ 0x7FFFFFFF, bits)


def _dsa_kernel(q_ref, k_ref, vt_ref, iq_ref, ik_ref, iwt_ref, o_ref, key_sc, mask_sc, qm_sc, acc_sc, *,
                tq, tk, seq, topk, slopes):
    qi = pl.program_id(1)
    q_pos0 = qi * tq
    nkb = (q_pos0 + tq + tk - 1) // tk
    n_tail = seq - nkb * tk
    t_row = q_pos0 + lax.broadcasted_iota(I32, (1, tq), 1)
    chunk_end = ((t_row >> CHUNK_SHIFT) + 1) << CHUNK_SHIFT
    s_col = lax.broadcasted_iota(I32, (tk, tq), 0)

    iq = iq_ref[0].astype(BF16)
    iq_heads = [iq[:, j * IDX_DIM:(j + 1) * IDX_DIM] for j in range(IDX_HEADS)]
    iw_rows = [iwt_ref[j:j + 1, :] for j in range(IDX_HEADS)]

    def score_block(kb, _):
        off = pl.multiple_of(kb * tk, tk)
        ik = ik_ref[0, pl.ds(off, tk), :].astype(BF16)
        acc = jnp.zeros((tk, tq), F32)
        for j in range(IDX_HEADS):
            acc = acc + iw_rows[j] * jnp.maximum(_dot_nt(ik, iq_heads[j]), 0.0)
        key_sc[kb] = _sort_key(jnp.where(off + s_col < chunk_end, acc, NEG_INF))
        return 0

    lax.fori_loop(0, nkb, score_block, 0)

    def count(pred):
        def body(kb, c):
            return c + jnp.sum(pred(key_sc[kb]).astype(I32), axis=0, keepdims=True)
        return lax.fori_loop(0, nkb, body, jnp.zeros((1, tq), I32))

    bits_per_check = 4

    def searching(st):
        i, _, done = st
        return jnp.logical_and(i < 32, jnp.min(done) == 0)

    def bit_steps(st):
        i, r, done = st
        for u in range(bits_per_check):
            cand = jnp.where(done > 0, r, r ^ (jnp.int32(1) << (31 - i - u)))
            cnt = count(lambda kk: kk >= cand) + jnp.where(cand <= KEY_NEG, n_tail, 0)
            r = jnp.where(cnt >= topk, cand, r)
            done = jnp.where(cnt == topk, 1, done)
        return i + bits_per_check, r, done

    _, tau, _ = lax.while_loop(
        searching, bit_steps, (jnp.int32(0), jnp.full((1, tq), -2 ** 31, I32), jnp.zeros((1, tq), I32)))
    cnt_gt = count(lambda kk: kk > tau) + jnp.where(tau < KEY_NEG, n_tail, 0)
    cnt_eq = count(lambda kk: kk == tau) + jnp.where(tau == KEY_NEG, n_tail, 0)
    need = topk - cnt_gt
    has_ties = jnp.max(cnt_eq - need) > 0

    @pl.when(jnp.logical_not(has_ties))
    def _():
        def body(kb, _):
            sel = jnp.logical_and(key_sc[kb] >= tau, kb * tk + s_col < chunk_end)
            mask_sc[kb] = jnp.where(sel, 0.0, NEG_INF)
            return 0
        lax.fori_loop(0, nkb, body, 0)

    @pl.when(has_ties)
    def _():
        r_i = lax.broadcasted_iota(I32, (tk, tk), 0)
        c_i = lax.broadcasted_iota(I32, (tk, tk), 1)
        earlier = jnp.where(c_i < r_i, 1.0, 0.0).astype(BF16)

        def body(kb, seen):
            kk = key_sc[kb]
            eq = kk == tau
            eqf = jnp.where(eq, 1.0, 0.0)
            rank = seen + jnp.dot(earlier, eqf.astype(BF16), preferred_element_type=F32)
            take = jnp.logical_and(eq, rank < need.astype(F32))
            sel = jnp.logical_and(jnp.logical_or(kk > tau, take), kb * tk + s_col < chunk_end)
            mask_sc[kb] = jnp.where(sel, 0.0, NEG_INF)
            return seen + jnp.sum(eqf, axis=0, keepdims=True)
        lax.fori_loop(0, nkb, body, jnp.zeros((1, tq), F32))

    hdim = DSA_HEAD_DIM
    pair = 2 * hdim
    n_heads = len(slopes)
    lane = lax.broadcasted_iota(I32, (tq, pair), 1)
    for p in range(n_heads // 2):
        qs = q_ref[0, :, p * pair:(p + 1) * pair] * (hdim ** -0.5 * LOG2E)
        zero = jnp.zeros_like(qs)
        qm_sc[2 * p] = jnp.where(lane < hdim, qs, zero)
        qm_sc[2 * p + 1] = jnp.where(lane >= hdim, qs, zero)
    acc_sc[...] = jnp.zeros_like(acc_sc)

    def attend(kb, carry):
        off = pl.multiple_of(kb * tk, tk)
        dist = jnp.abs(t_row - (off + s_col)).astype(F32)
        madd = mask_sc[kb]

        def qk(hd):
            k = k_ref[0, pl.ds(off, tk), (hd // 2) * pair:(hd // 2 + 1) * pair]
            return _dot_nt(k, qm_sc[hd])

        def pv(hd, a, pr):
            rows = slice(hd * hdim, (hd + 1) * hdim)
            acc_sc[rows, :] = a * acc_sc[rows, :] + jnp.dot(vt_ref[kb, rows, :], pr, preferred_element_type=F32)

        out = []
        s_next = qk(0)
        pending = None
        for hd in range(n_heads):
            s = s_next - (slopes[hd] * LOG2E) * dist + madd
            if hd + 1 < n_heads:
                s_next = qk(hd + 1)
            m, l = carry[2 * hd], carry[2 * hd + 1]
            m_new = jnp.maximum(m, jnp.max(s, axis=0, keepdims=True))
            pr = jnp.exp2(s - m_new)
            a = jnp.exp2(m - m_new)
            out.extend((m_new, a * l + jnp.sum(pr, axis=0, keepdims=True)))
            if pending is not None:
                pv(*pending)
            pending = (hd, a, pr.astype(BF16))
        pv(*pending)
        return tuple(out)

    m0 = jnp.full((1, tq), -jnp.inf, F32)
    l0 = jnp.zeros((1, tq), F32)
    carry = lax.fori_loop(0, nkb, attend, (m0, l0) * n_heads)
    for hd in range(n_heads):
        rows = slice(hd * hdim, (hd + 1) * hdim)
        acc_sc[rows, :] = acc_sc[rows, :] / carry[2 * hd + 1]
    o_ref[0] = acc_sc[...].T.astype(o_ref.dtype)


def _dsa_attention(main, vt, iq, ik, iwt, n_heads, col_q, col_k, vrow0, tq=512):
    B, S, _ = main.shape
    tk = vt.shape[-1]
    width = n_heads * DSA_HEAD_DIM
    nq = S // tq
    topk = min(IDX_TOPK_MAX, S // 4)
    slopes = tuple(2.0 ** (-8.0 * (i + 1) / n_heads) for i in range(n_heads))
    qrow = lambda n, c: pl.BlockSpec((1, tq, n), lambda b, q: (b, q, c))
    seqb = lambda n, c: pl.BlockSpec((1, S, n), lambda b, q: (b, 0, c))
    return pl.pallas_call(
        functools.partial(_dsa_kernel, tq=tq, tk=tk, seq=S, topk=topk, slopes=slopes),
        grid=(B, nq),
        in_specs=[qrow(width, col_q), seqb(width, col_k),
                  pl.BlockSpec((S // tk, width, tk), lambda b, q: (b, vrow0, 0)),
                  qrow(iq.shape[-1], 0), seqb(ik.shape[-1], 0),
                  pl.BlockSpec((iwt.shape[0], tq), lambda b, q: (0, b * nq + q))],
        out_specs=qrow(width, 0),
        out_shape=jax.ShapeDtypeStruct((B, S, width), BF16),
        scratch_shapes=[pltpu.VMEM((S // tk, tk, tq), I32), pltpu.VMEM((S // tk, tk, tq), F32),
                        pltpu.VMEM((n_heads, tq, 2 * DSA_HEAD_DIM), BF16),
                        pltpu.VMEM((width, tq), F32)],
        compiler_params=_cparams(2), name="dsa_attn")(main, main, vt, iq, ik, iwt)


def _outproj_kernel(a_ref, b_ref, h_ref, w_ref, g_ref, beta_ref, o_ref, *, alpha):
    wa = a_ref.shape[1]
    y = jnp.dot(a_ref[...], w_ref[:wa, :], preferred_element_type=F32)
    y = y + jnp.dot(b_ref[...], w_ref[wa:, :], preferred_element_type=F32)
    o_ref[...] = _layer_norm(alpha * h_ref[...] + y, g_ref[...], beta_ref[...])


def _output_projection(a, b, h, w_out, g, beta, alpha, tm=512):
    T, D = h.shape
    rows = lambda n: pl.BlockSpec((tm, n), lambda t: (t, 0))
    vec = pl.BlockSpec((1, D), lambda t: (0, 0))
    return pl.pallas_call(
        functools.partial(_outproj_kernel, alpha=alpha), grid=(T // tm,),
        in_specs=[rows(a.shape[1]), rows(b.shape[1]), rows(D),
                  pl.BlockSpec(w_out.shape, lambda t: (0, 0)), vec, vec],
        out_specs=rows(D), out_shape=jax.ShapeDtypeStruct((T, D), F32),
        compiler_params=_cparams(1), name="out_proj_ln1")(
            a, b, h, w_out, g.reshape(1, D), beta.reshape(1, D))


def _top_rows(x, n):
    rows = []
    rank = jnp.full(x.shape, float(n), F32)
    for b in range(n):
        mx = jnp.max(x, axis=0, keepdims=True)
        rows.append(mx)
        hit = x == mx
        rank = jnp.where(hit, float(b), rank)
        x = jnp.where(hit, -jnp.inf, x)
    return rows, rank


def _route_kernel(h_ref, wq_ref, sk_ref, n1_ref, c1_ref, r2_ref, e2_ref):
    hb = h_ref[...].astype(BF16)
    qd = 2 * PEER_HALF_DIM
    k = PEER_TOPK
    for hd in range(PEER_HEADS):
        q = jnp.dot(hb, wq_ref[:, hd * qd:(hd + 1) * qd], preferred_element_type=F32).astype(BF16)
        s1 = _dot_nt(sk_ref[hd, 0], q[:, :PEER_HALF_DIM])
        s2 = _dot_nt(sk_ref[hd, 1], q[:, PEER_HALF_DIM:])
        top1, _ = _top_rows(s1, k)
        top2, rank2 = _top_rows(s2, k)
        top2 = jnp.concatenate(top2, axis=0)
        half = k // 2
        cand = [top1[0] + top2] + [top1[a] + top2[:half] for a in range(1, half)]
        cand.append(jnp.concatenate(top1[half:], axis=0) + top2[0:1])
        best, _ = _top_rows(jnp.concatenate(cand, axis=0), k + 1)
        z = jnp.ones_like(best[0])
        for r in best[1:k]:
            z = z + jnp.exp(r - best[0])
        cut = 0.5 * (best[k - 1] + best[k])
        n1 = jnp.zeros_like(s1)
        for b in range(k):
            n1 = n1 + jnp.where(s1 >= cut - top2[b:b + 1], 1.0, 0.0)
        n1_ref[hd] = n1
        c1_ref[hd] = jnp.where(s1 >= top1[-1], jnp.exp(s1 - top1[0]) / z, 0.0)
        r2_ref[hd] = rank2.astype(BF16)
        e2_ref[hd] = jnp.where(s2 >= top2[k - 1:], jnp.exp(s2 - top2[0:1]), 0.0).astype(BF16)


def _peer_route(h, wq, subkeys, tm=256):
    T, D = h.shape
    shape = (PEER_HEADS, PEER_N_KEYS, T)
    ospec = pl.BlockSpec((PEER_HEADS, PEER_N_KEYS, tm), lambda t: (0, 0, t))
    return pl.pallas_call(
        _route_kernel, grid=(T // tm,),
        in_specs=[pl.BlockSpec((tm, D), lambda t: (t, 0)),
                  pl.BlockSpec(wq.shape, lambda t: (0, 0)),
                  pl.BlockSpec(subkeys.shape, lambda t: (0, 0, 0, 0))],
        out_specs=[ospec] * 4,
        out_shape=[jax.ShapeDtypeStruct(shape, F32), jax.ShapeDtypeStruct(shape, F32),
                   jax.ShapeDtypeStruct(shape, BF16), jax.ShapeDtypeStruct(shape, BF16)],
        compiler_params=_cparams(1), name="peer_route")(h, wq, subkeys)


def _gelu(x):
    return 0.5 * x * (1.0 + lax.erf(x * (2.0 ** -0.5)))


PEER_TILE = 1024
MXU_DIM = 256
GATE_ROWS = 16


def _peer_kernel(h_ref, u_ref, vt_ref, n1_ref, c1_ref, r2_ref, e2_ref, o_ref,
                 xb_sc, acc_sc, act_a, act_b, w_a, w_b, r2_sc, e2_sc, *, te, n_tiles):
    s = pl.program_id(1)
    tm, d = h_ref.shape

    @pl.when(s == 0)
    def _():
        xb_sc[...] = h_ref[...].astype(BF16)
        acc_sc[...] = jnp.zeros_like(acc_sc)
        r2_sc[...] = r2_ref[...]
        e2_sc[...] = e2_ref[...]

    per = te // PEER_N_KEYS

    def step(act_new, act_old, w_new, w_old, stages):
        tile = s - 1

        def gate_piece(ii, lt):
            i = tile * per + ii
            lanes = slice(lt * LANES, (lt + 1) * LANES)
            nb = [jnp.broadcast_to(n1_ref[hd, pl.ds(i, 1), :][:, lanes].astype(BF16), (GATE_ROWS, LANES))
                  for hd in range(PEER_HEADS)]
            cb = [jnp.broadcast_to(c1_ref[hd, pl.ds(i, 1), :][:, lanes].astype(BF16), (GATE_ROWS, LANES))
                  for hd in range(PEER_HEADS)]
            zero = jnp.zeros((GATE_ROWS, LANES), BF16)
            for r0 in range(0, PEER_N_KEYS, GATE_ROWS):
                rows = slice(r0, r0 + GATE_ROWS)
                g = zero
                for hd in range(PEER_HEADS):
                    g = g + jnp.where(r2_sc[hd, rows, lanes] < nb[hd], e2_sc[hd, rows, lanes] * cb[hd], zero)
                arow = slice(ii * PEER_N_KEYS + r0, ii * PEER_N_KEYS + r0 + GATE_ROWS)
                w_new[arow, lanes] = g * act_old[arow, lanes]

        def stage1_piece(nc, rc):
            cols = slice(nc * MXU_DIM, (nc + 1) * MXU_DIM)
            rows = slice(rc * MXU_DIM, (rc + 1) * MXU_DIM)
            act_new[rows, cols] = _gelu(_dot_nt(u_ref[rows, :], xb_sc[cols, :])).astype(BF16)

        def stage3_piece(nc, mc):
            cols = slice(nc * MXU_DIM, (nc + 1) * MXU_DIM)
            rr = slice(mc * MXU_DIM, (mc + 1) * MXU_DIM)
            acc_sc[rr, cols] += jnp.dot(vt_ref[rr, :], w_old[:, cols], preferred_element_type=F32)

        gates, mxu = [], []
        if 2 in stages:
            gates = [functools.partial(gate_piece, ii, lt) for ii in range(per) for lt in range(tm // LANES)]
        if 1 in stages:
            mxu += [functools.partial(stage1_piece, nc, rc) for nc in range(tm // MXU_DIM) for rc in range(te // MXU_DIM)]
        if 3 in stages:
            mxu += [functools.partial(stage3_piece, nc, mc) for nc in range(tm // MXU_DIM) for mc in range(d // MXU_DIM)]
        n_slots = max(len(gates), len(mxu))
        for b in range(n_slots):
            for pieces in (mxu, gates):
                first = -(-b * len(pieces) // n_slots)
                for piece in pieces[first:-(-(b + 1) * len(pieces) // n_slots)]:
                    piece()

    act, w = (act_a, act_b), (w_a, w_b)
    steady = jnp.logical_and(s >= 2, s < n_tiles)
    variants = [(s == 0, 0, (1,)), (s == 1, 1, (1, 2)),
                (jnp.logical_and(steady, s % 2 == 0), 0, (1, 2, 3)),
                (jnp.logical_and(steady, s % 2 == 1), 1, (1, 2, 3)),
                (s == n_tiles, n_tiles % 2, (2, 3)), (s == n_tiles + 1, (n_tiles + 1) % 2, (3,))]
    for cond, par, stages in variants:
        @pl.when(cond)
        def _(par=par, stages=stages):
            step(act[par], act[1 - par], w[1 - par], w[par], stages)

    @pl.when(s == n_tiles + 1)
    def _():
        o_ref[...] = acc_sc[...].T


def _peer_experts(h, u_bf, vt_tiles, n1, c1, r2, e2, tm=512):
    T, D = h.shape
    n_tiles, _, te = vt_tiles.shape
    rspec = pl.BlockSpec((PEER_HEADS, PEER_N_KEYS, tm), lambda t, s: (0, 0, t))
    return pl.pallas_call(
        functools.partial(_peer_kernel, te=te, n_tiles=n_tiles), grid=(T // tm, n_tiles + 2),
        in_specs=[pl.BlockSpec((tm, D), lambda t, s: (t, 0)),
                  pl.BlockSpec((te, D), lambda t, s: (jnp.minimum(s, n_tiles - 1), 0)),
                  pl.BlockSpec((None, D, te), lambda t, s: (jnp.clip(s - 2, 0, n_tiles - 1), 0, 0)),
                  rspec, rspec, rspec, rspec],
        out_specs=pl.BlockSpec((tm, D), lambda t, s: (t, 0)),
        out_shape=jax.ShapeDtypeStruct((T, D), F32),
        scratch_shapes=[pltpu.VMEM((tm, D), BF16), pltpu.VMEM((D, tm), F32),
                        pltpu.VMEM((te, tm), BF16), pltpu.VMEM((te, tm), BF16),
                        pltpu.VMEM((te, tm), BF16), pltpu.VMEM((te, tm), BF16),
                        pltpu.VMEM((PEER_HEADS, PEER_N_KEYS, tm), BF16), pltpu.VMEM((PEER_HEADS, PEER_N_KEYS, tm), BF16)],
        compiler_params=_cparams(2), name="peer_experts")(h, u_bf, vt_tiles, n1, c1, r2, e2)


def _ple_kernel(h_ref, f_ref, p_ref, wg_ref, bg_ref, wp_ref, g_ref, beta_ref, o_ref, *, alpha):
    r = alpha * h_ref[...] + f_ref[...]
    z = jnp.dot(r.astype(BF16), wg_ref[...], preferred_element_type=F32) + bg_ref[...]
    gate = 1.0 / (1.0 + jnp.exp(-z))
    r = r + gate * jnp.dot(p_ref[...].astype(BF16), wp_ref[...], preferred_element_type=F32)
    o_ref[...] = _layer_norm(r, g_ref[...], beta_ref[...])


def _ple_ln2(h, f, p, layer, wg, bg, wp, g, beta, alpha, tm=512):
    T, D = h.shape
    pd = p.shape[-1]
    rows = pl.BlockSpec((tm, D), lambda t: (t, 0))
    vec = pl.BlockSpec((1, D), lambda t: (0, 0))
    return pl.pallas_call(
        functools.partial(_ple_kernel, alpha=alpha), grid=(T // tm,),
        in_specs=[rows, rows, pl.BlockSpec((None, tm, pd), lambda t: (layer, t, 0)),
                  pl.BlockSpec(wg.shape, lambda t: (0, 0)), vec,
                  pl.BlockSpec(wp.shape, lambda t: (0, 0)), vec, vec],
        out_specs=rows, out_shape=jax.ShapeDtypeStruct((T, D), F32),
        compiler_params=_cparams(1), name="ple_ln2")(
            h, f, p, wg, bg.reshape(1, D), wp, g.reshape(1, D), beta.reshape(1, D))


def _mixer_layer(h, B, S, w_in, w_out, lam_vecs, subln_g, lam_init, ln_g, ln_b, alpha):
    T, D = h.shape
    dw = D // 2
    n_diff = dw // (2 * DIFF_HEAD_DIM)
    sw = D - dw
    n_dsa = sw // DSA_HEAD_DIM
    n_iq = IDX_HEADS * IDX_DIM
    wb = w_in.astype(BF16)
    c_sv = 3 * dw + 2 * sw
    c_iq = c_sv + sw
    w_main = wb[:, :c_sv]
    w_vt = wb[:, c_sv:c_iq].T
    w_iq = wb[:, c_iq:c_iq + n_iq]
    w_ik = wb[:, c_iq + n_iq:c_iq + n_iq + IDX_DIM]
    w_iwt = jnp.pad(wb[:, c_iq + n_iq + IDX_DIM:].T, ((0, 8 - IDX_HEADS), (0, 0)))
    main, vt, iq, ik, iwt = _input_projection(h, w_main, w_vt, w_iq, w_ik, w_iwt)
    main = main.reshape(B, S, c_sv)
    a = _diff_attention(main, lam_vecs, subln_g, lam_init, n_diff, 0)
    b = _dsa_attention(main, vt, iq.reshape(B, S, -1), ik.reshape(B, S, -1), iwt, n_dsa,
                       3 * dw // sw, 3 * dw // sw + 1, 0)
    return _output_projection(a.reshape(T, -1), b.reshape(T, -1), h, w_out.astype(BF16), ln_g, ln_b, alpha)


def kernel(x, p, ln_in_g, ln_in_b, w_in, w_out, diff_lambda, diff_subln_g, ln1_g, ln1_b, peer_wq, peer_subkeys, peer_u, peer_v, ple_wg, ple_bg, ple_wp, ln2_g, ln2_b):
    B, S, D = x.shape
    depth = w_in.shape[0]
    T = B * S
    alpha = (2 * depth) ** 0.25
    h = _entry_layer_norm(x.reshape(T, D), ln_in_g, ln_in_b)
    p2 = p.reshape(depth, T, p.shape[-1])
    for i in range(depth):
        lam_init = 0.8 - 0.6 * math.exp(-0.3 * i)
        h = _mixer_layer(h, B, S, w_in[i], w_out[i], diff_lambda[i], diff_subln_g[i], lam_init,
                         ln1_g[i], ln1_b[i], alpha)
        n1, c1, r2, e2 = _peer_route(h, peer_wq[i].astype(BF16), peer_subkeys[i].astype(BF16))
        vt_tiles = peer_v[i].astype(BF16).reshape(-1, PEER_TILE, D).transpose(0, 2, 1)
        f = _peer_experts(h, peer_u[i].astype(BF16), vt_tiles, n1, c1, r2, e2)
        h = _ple_ln2(h, f, p2, i, ple_wg[i].astype(BF16), ple_bg[i], ple_wp[i].astype(BF16),
                     ln2_g[i], ln2_b[i], alpha)
    return h.reshape(B, S, D)
```

```python
import functools
import math
import struct

import jax
import jax.numpy as jnp
from jax import lax
from jax.experimental import pallas as pl
from jax.experimental.pallas import tpu as pltpu

F32 = jnp.float32
BF16 = jnp.bfloat16
I32 = jnp.int32

LN_EPS = 1e-5
NEG_INF = -1e30
CHUNK = 64
CHUNK_SHIFT = 6
LOG2E = math.log2(math.e)
LANES = 128
VMEM_LIMIT = 56 * 1024 * 1024

DIFF_HEAD_DIM = 64
DSA_HEAD_DIM = 64
IDX_HEADS = 4
IDX_DIM = 64
IDX_TOPK_MAX = 256
PEER_HEADS = 8
PEER_N_KEYS = 128
PEER_HALF_DIM = 128
PEER_TOPK = 16

_NEG_BITS = struct.unpack("<i", struct.pack("<f", NEG_INF))[0]
KEY_NEG = _NEG_BITS ^ 0x7FFFFFFF


def _cparams(n_axes):
    return pltpu.CompilerParams(
        dimension_semantics=("arbitrary",) * n_axes, vmem_limit_bytes=VMEM_LIMIT)


def _dot_nt(a, b):
    return lax.dot_general(a, b, (((1,), (1,)), ((), ())), preferred_element_type=F32)


def _layer_norm(x, g, b):
    mu = jnp.mean(x, axis=-1, keepdims=True)
    xc = x - mu
    var = jnp.mean(xc * xc, axis=-1, keepdims=True)
    return xc * lax.rsqrt(var + LN_EPS) * g + b


def _ln_kernel(x_ref, g_ref, b_ref, o_ref):
    o_ref[...] = _layer_norm(x_ref[...], g_ref[...], b_ref[...])


def _entry_layer_norm(x, g, b, tm=512):
    T, D = x.shape
    row = pl.BlockSpec((tm, D), lambda t: (t, 0))
    vec = pl.BlockSpec((1, D), lambda t: (0, 0))
    return pl.pallas_call(
        _ln_kernel, grid=(T // tm,), in_specs=[row, vec, vec], out_specs=row,
        out_shape=jax.ShapeDtypeStruct((T, D), F32), compiler_params=_cparams(1),
        name="entry_ln")(x, g.reshape(1, D), b.reshape(1, D))


def _inproj_kernel(h_ref, wm_ref, wvt_ref, wiq_ref, wik_ref, wiwt_ref, main_ref, vt_ref, iq_ref, ik_ref, iwt_ref, *, tn):
    hb = h_ref[...].astype(BF16)
    for j in range(0, wm_ref.shape[1], tn):
        main_ref[:, j:j + tn] = jnp.dot(hb, wm_ref[:, j:j + tn], preferred_element_type=F32).astype(BF16)
    vt_ref[0] = _dot_nt(wvt_ref[...], hb).astype(BF16)
    iq_ref[...] = jnp.dot(hb, wiq_ref[...], preferred_element_type=F32) * (IDX_DIM ** -0.5)
    ik_ref[...] = jnp.dot(hb, wik_ref[...], preferred_element_type=F32)
    iwt_ref[...] = _dot_nt(wiwt_ref[...], hb) * (IDX_HEADS ** -0.5)


def _input_projection(h, w_main, w_vt, w_iq, w_ik, w_iwt, tm=512, tn=512):
    T, D = h.shape
    nm = w_main.shape[1]
    nv = w_vt.shape[0]

    def full(w):
        return pl.BlockSpec(w.shape, lambda t: (0, 0))

    def rows(n):
        return pl.BlockSpec((tm, n), lambda t: (t, 0))

    return pl.pallas_call(
        functools.partial(_inproj_kernel, tn=tn), grid=(T // tm,),
        in_specs=[rows(D), full(w_main), full(w_vt), full(w_iq), full(w_ik), full(w_iwt)],
        out_specs=[rows(nm), pl.BlockSpec((1, nv, tm), lambda t: (t, 0, 0)),
                   rows(w_iq.shape[1]), rows(w_ik.shape[1]),
                   pl.BlockSpec((w_iwt.shape[0], tm), lambda t: (0, t))],
        out_shape=[jax.ShapeDtypeStruct((T, nm), BF16),
                   jax.ShapeDtypeStruct((T // tm, nv, tm), BF16),
                   jax.ShapeDtypeStruct((T, w_iq.shape[1]), F32),
                   jax.ShapeDtypeStruct((T, w_ik.shape[1]), F32),
                   jax.ShapeDtypeStruct((w_iwt.shape[0], T), F32)],
        compiler_params=_cparams(1), name="in_proj")(h, w_main, w_vt, w_iq, w_ik, w_iwt)


def _diff_kernel(lamv_ref, g_ref, q_ref, k_ref, v_ref, o_ref, qm_sc, acc_sc, *, tq, tk, lam_init, slopes):
    qi = pl.program_id(1)
    dv = 2 * DIFF_HEAD_DIM
    n_heads = len(slopes)
    lane = lax.broadcasted_iota(I32, (tq, dv), 1)
    for hd in range(n_heads):
        qs = q_ref[0, :, hd * dv:(hd + 1) * dv] * (DIFF_HEAD_DIM ** -0.5 * LOG2E)
        zero = jnp.zeros_like(qs)
        qm_sc[2 * hd] = jnp.where(lane < DIFF_HEAD_DIM, qs, zero)
        qm_sc[2 * hd + 1] = jnp.where(lane >= DIFF_HEAD_DIM, qs, zero)
    acc_sc[...] = jnp.zeros_like(acc_sc)
    t_idx = qi * tq + lax.broadcasted_iota(I32, (tq, tk), 0)

    def step(kb, carry, diagonal):
        off = pl.multiple_of(kb * tk, tk)
        s_idx = off + lax.broadcasted_iota(I32, (tq, tk), 1)
        dist = jnp.abs(t_idx - s_idx).astype(F32)
        if diagonal:
            visible = s_idx < (((t_idx >> CHUNK_SHIFT) + 1) << CHUNK_SHIFT)

        def qk(c):
            return _dot_nt(qm_sc[c], k_ref[0, pl.ds(off, tk), (c // 2) * dv:(c // 2 + 1) * dv])

        def pv(c, a, p):
            v = v_ref[0, pl.ds(off, tk), (c // 2) * dv:(c // 2 + 1) * dv]
            acc_sc[c] = a * acc_sc[c] + jnp.dot(p, v, preferred_element_type=F32)

        out = []
        s_next = qk(0)
        pending = None
        for c in range(2 * n_heads):
            s = s_next - (slopes[c // 2] * LOG2E) * dist
            if c + 1 < 2 * n_heads:
                s_next = qk(c + 1)
            if diagonal:
                s = jnp.where(visible, s, NEG_INF)
            m, l = carry[2 * c], carry[2 * c + 1]
            m_new = jnp.maximum(m, jnp.max(s, axis=-1, keepdims=True))
            p = jnp.exp2(s - m_new)
            a = jnp.exp2(m - m_new)
            out.extend((m_new, a * l + jnp.sum(p, axis=-1, keepdims=True)))
            if pending is not None:
                pv(*pending)
            pending = (c, a, p.astype(BF16))
        pv(*pending)
        return tuple(out)

    m0 = jnp.full((tq, 1), -jnp.inf, F32)
    l0 = jnp.zeros((tq, 1), F32)
    n_full = (qi * tq) // tk
    carry = lax.fori_loop(0, n_full, functools.partial(step, diagonal=False), (m0, l0) * (2 * n_heads))
    for d in range(max(tq // tk, 1)):
        carry = step(n_full + d, carry, True)

    lv = lamv_ref[...]
    lam = (jnp.exp(jnp.sum(lv[0:1] * lv[1:2], axis=-1, keepdims=True))
           - jnp.exp(jnp.sum(lv[2:3] * lv[3:4], axis=-1, keepdims=True)) + lam_init)
    for hd in range(n_heads):
        o = acc_sc[2 * hd] / carry[4 * hd + 1] - lam * (acc_sc[2 * hd + 1] / carry[4 * hd + 3])
        o = o * lax.rsqrt(jnp.mean(o * o, axis=-1, keepdims=True) + LN_EPS) * g_ref[...]
        o_ref[0, :, hd * dv:(hd + 1) * dv] = (o * (1.0 - lam_init)).astype(o_ref.dtype)


def _diff_attention(main, lam_vecs, subln_g, lam_init, n_heads, col0, tq=256, tk=512):
    B, S, _ = main.shape
    dv = 2 * DIFF_HEAD_DIM
    width = n_heads * dv
    slopes = tuple(2.0 ** (-8.0 * (i + 1) / n_heads) for i in range(n_heads))
    kv = lambda c: pl.BlockSpec((1, S, width), lambda b, q: (b, 0, c))
    qo = lambda c: pl.BlockSpec((1, tq, width), lambda b, q: (b, q, c))
    return pl.pallas_call(
        functools.partial(_diff_kernel, tq=tq, tk=tk, lam_init=lam_init, slopes=slopes),
        grid=(B, S // tq),
        in_specs=[pl.BlockSpec(lam_vecs.shape, lambda b, q: (0, 0)),
                  pl.BlockSpec((1, dv), lambda b, q: (0, 0)),
                  qo(col0), kv(col0 + 1), kv(col0 + 2)],
        out_specs=qo(0),
        out_shape=jax.ShapeDtypeStruct((B, S, width), BF16),
        scratch_shapes=[pltpu.VMEM((2 * n_heads, tq, dv), BF16), pltpu.VMEM((2 * n_heads, tq, dv), F32)],
        compiler_params=_cparams(2), name="diff_attn")(
            lam_vecs, subln_g.reshape(1, dv), main, main, main)


def _sort_key(x):
    bits = pltpu.bitcast(x + 0.0, I32)
    return jnp.where(bits < 0, bits ^ 0x7FFFFFFF, bits)


def _dsa_kernel(q_ref, k_ref, vt_ref, iq_ref, ik_ref, iwt_ref, o_ref, key_sc, mask_sc, qm_sc, acc_sc, *,
                tq, tk, seq, topk, slopes):
    qi = pl.program_id(1)
    q_pos0 = qi * tq
    nkb = (q_pos0 + tq + tk - 1) // tk
    n_tail = seq - nkb * tk
    t_row = q_pos0 + lax.broadcasted_iota(I32, (1, tq), 1)
    chunk_end = ((t_row >> CHUNK_SHIFT) + 1) << CHUNK_SHIFT
    s_col = lax.broadcasted_iota(I32, (tk, tq), 0)

    iq = iq_ref[0].astype(BF16)
    iq_heads = [iq[:, j * IDX_DIM:(j + 1) * IDX_DIM] for j in range(IDX_HEADS)]
    iw_rows = [iwt_ref[j:j + 1, :] for j in range(IDX_HEADS)]

    def score_block(kb, _):
        off = pl.multiple_of(kb * tk, tk)
        ik = ik_ref[0, pl.ds(off, tk), :].astype(BF16)
        acc = jnp.zeros((tk, tq), F32)
        for j in range(IDX_HEADS):
            acc = acc + iw_rows[j] * jnp.maximum(_dot_nt(ik, iq_heads[j]), 0.0)
        key_sc[kb] = _sort_key(jnp.where(off + s_col < chunk_end, acc, NEG_INF))
        return 0

    lax.fori_loop(0, nkb, score_block, 0)

    def count(pred):
        def body(kb, c):
            return c + jnp.sum(pred(key_sc[kb]).astype(I32), axis=0, keepdims=True)
        return lax.fori_loop(0, nkb, body, jnp.zeros((1, tq), I32))

    bits_per_check = 4

    def searching(st):
        i, _, done = st
        return jnp.logical_and(i < 32, jnp.min(done) == 0)

    def bit_steps(st):
        i, r, done = st
        for u in range(bits_per_check):
            cand = jnp.where(done > 0, r, r ^ (jnp.int32(1) << (31 - i - u)))
            cnt = count(lambda kk: kk >= cand) + jnp.where(cand <= KEY_NEG, n_tail, 0)
            r = jnp.where(cnt >= topk, cand, r)
            done = jnp.where(cnt == topk, 1, done)
        return i + bits_per_check, r, done

    _, tau, _ = lax.while_loop(
        searching, bit_steps, (jnp.int32(0), jnp.full((1, tq), -2 ** 31, I32), jnp.zeros((1, tq), I32)))
    cnt_gt = count(lambda kk: kk > tau) + jnp.where(tau < KEY_NEG, n_tail, 0)
    cnt_eq = count(lambda kk: kk == tau) + jnp.where(tau == KEY_NEG, n_tail, 0)
    need = topk - cnt_gt
    has_ties = jnp.max(cnt_eq - need) > 0

    @pl.when(jnp.logical_not(has_ties))
    def _():
        def body(kb, _):
            sel = jnp.logical_and(key_sc[kb] >= tau, kb * tk + s_col < chunk_end)
            mask_sc[kb] = jnp.where(sel, 0.0, NEG_INF)
            return 0
        lax.fori_loop(0, nkb, body, 0)

    @pl.when(has_ties)
    def _():
        r_i = lax.broadcasted_iota(I32, (tk, tk), 0)
        c_i = lax.broadcasted_iota(I32, (tk, tk), 1)
        earlier = jnp.where(c_i < r_i, 1.0, 0.0).astype(BF16)

        def body(kb, seen):
            kk = key_sc[kb]
            eq = kk == tau
            eqf = jnp.where(eq, 1.0, 0.0)
            rank = seen + jnp.dot(earlier, eqf.astype(BF16), preferred_element_type=F32)
            take = jnp.logical_and(eq, rank < need.astype(F32))
            sel = jnp.logical_and(jnp.logical_or(kk > tau, take), kb * tk + s_col < chunk_end)
            mask_sc[kb] = jnp.where(sel, 0.0, NEG_INF)
            return seen + jnp.sum(eqf, axis=0, keepdims=True)
        lax.fori_loop(0, nkb, body, jnp.zeros((1, tq), F32))

    hdim = DSA_HEAD_DIM
    pair = 2 * hdim
    n_heads = len(slopes)
    lane = lax.broadcasted_iota(I32, (tq, pair), 1)
    for p in range(n_heads // 2):
        qs = q_ref[0, :, p * pair:(p + 1) * pair] * (hdim ** -0.5 * LOG2E)
        zero = jnp.zeros_like(qs)
        qm_sc[2 * p] = jnp.where(lane < hdim, qs, zero)
        qm_sc[2 * p + 1] = jnp.where(lane >= hdim, qs, zero)
    acc_sc[...] = jnp.zeros_like(acc_sc)

    def attend(kb, carry):
        off = pl.multiple_of(kb * tk, tk)
        dist = jnp.abs(t_row - (off + s_col)).astype(F32)
        madd = mask_sc[kb]

        def qk(hd):
            k = k_ref[0, pl.ds(off, tk), (hd // 2) * pair:(hd // 2 + 1) * pair]
            return _dot_nt(k, qm_sc[hd])

        def pv(hd, a, pr):
            rows = slice(hd * hdim, (hd + 1) * hdim)
            acc_sc[rows, :] = a * acc_sc[rows, :] + jnp.dot(vt_ref[kb, rows, :], pr, preferred_element_type=F32)

        out = []
        s_next = qk(0)
        pending = None
        for hd in range(n_heads):
            s = s_next - (slopes[hd] * LOG2E) * dist + madd
            if hd + 1 < n_heads:
                s_next = qk(hd + 1)
            m, l = carry[2 * hd], carry[2 * hd + 1]
            m_new = jnp.maximum(m, jnp.max(s, axis=0, keepdims=True))
            pr = jnp.exp2(s - m_new)
            a = jnp.exp2(m - m_new)
            out.extend((m_new, a * l + jnp.sum(pr, axis=0, keepdims=True)))
            if pending is not None:
                pv(*pending)
            pending = (hd, a, pr.astype(BF16))
        pv(*pending)
        return tuple(out)

    m0 = jnp.full((1, tq), -jnp.inf, F32)
    l0 = jnp.zeros((1, tq), F32)
    carry = lax.fori_loop(0, nkb, attend, (m0, l0) * n_heads)
    for hd in range(n_heads):
        rows = slice(hd * hdim, (hd + 1) * hdim)
        acc_sc[rows, :] = acc_sc[rows, :] / carry[2 * hd + 1]
    o_ref[0] = acc_sc[...].T.astype(o_ref.dtype)


def _dsa_attention(main, vt, iq, ik, iwt, n_heads, col_q, col_k, vrow0, tq=512):
    B, S, _ = main.shape
    tk = vt.shape[-1]
    width = n_heads * DSA_HEAD_DIM
    nq = S // tq
    topk = min(IDX_TOPK_MAX, S // 4)
    slopes = tuple(2.0 ** (-8.0 * (i + 1) / n_heads) for i in range(n_heads))
    qrow = lambda n, c: pl.BlockSpec((1, tq, n), lambda b, q: (b, q, c))
    seqb = lambda n, c: pl.BlockSpec((1, S, n), lambda b, q: (b, 0, c))
    return pl.pallas_call(
        functools.partial(_dsa_kernel, tq=tq, tk=tk, seq=S, topk=topk, slopes=slopes),
        grid=(B, nq),
        in_specs=[qrow(width, col_q), seqb(width, col_k),
                  pl.BlockSpec((S // tk, width, tk), lambda b, q: (b, vrow0, 0)),
                  qrow(iq.shape[-1], 0), seqb(ik.shape[-1], 0),
                  pl.BlockSpec((iwt.shape[0], tq), lambda b, q: (0, b * nq + q))],
        out_specs=qrow(width, 0),
        out_shape=jax.ShapeDtypeStruct((B, S, width), BF16),
        scratch_shapes=[pltpu.VMEM((S // tk, tk, tq), I32), pltpu.VMEM((S // tk, tk, tq), F32),
                        pltpu.VMEM((n_heads, tq, 2 * DSA_HEAD_DIM), BF16),
                        pltpu.VMEM((width, tq), F32)],
        compiler_params=_cparams(2), name="dsa_attn")(main, main, vt, iq, ik, iwt)


def _outproj_kernel(a_ref, b_ref, h_ref, w_ref, g_ref, beta_ref, o_ref, *, alpha):
    wa = a_ref.shape[1]
    y = jnp.dot(a_ref[...], w_ref[:wa, :], preferred_element_type=F32)
    y = y + jnp.dot(b_ref[...], w_ref[wa:, :], preferred_element_type=F32)
    o_ref[...] = _layer_norm(alpha * h_ref[...] + y, g_ref[...], beta_ref[...])


def _output_projection(a, b, h, w_out, g, beta, alpha, tm=512):
    T, D = h.shape
    rows = lambda n: pl.BlockSpec((tm, n), lambda t: (t, 0))
    vec = pl.BlockSpec((1, D), lambda t: (0, 0))
    return pl.pallas_call(
        functools.partial(_outproj_kernel, alpha=alpha), grid=(T // tm,),
        in_specs=[rows(a.shape[1]), rows(b.shape[1]), rows(D),
                  pl.BlockSpec(w_out.shape, lambda t: (0, 0)), vec, vec],
        out_specs=rows(D), out_shape=jax.ShapeDtypeStruct((T, D), F32),
        compiler_params=_cparams(1), name="out_proj_ln1")(
            a, b, h, w_out, g.reshape(1, D), beta.reshape(1, D))


def _top_rows(x, n):
    rows = []
    rank = jnp.full(x.shape, float(n), F32)
    for b in range(n):
        mx = jnp.max(x, axis=0, keepdims=True)
        rows.append(mx)
        hit = x == mx
        rank = jnp.where(hit, float(b), rank)
        x = jnp.where(hit, -jnp.inf, x)
    return rows, rank


def _route_kernel(h_ref, wq_ref, sk_ref, n1_ref, c1_ref, r2_ref, e2_ref):
    hb = h_ref[...].astype(BF16)
    qd = 2 * PEER_HALF_DIM
    k = PEER_TOPK
    for hd in range(PEER_HEADS):
        q = jnp.dot(hb, wq_ref[:, hd * qd:(hd + 1) * qd], preferred_element_type=F32).astype(BF16)
        s1 = _dot_nt(sk_ref[hd, 0], q[:, :PEER_HALF_DIM])
        s2 = _dot_nt(sk_ref[hd, 1], q[:, PEER_HALF_DIM:])
        top1, _ = _top_rows(s1, k)
        top2, rank2 = _top_rows(s2, k)
        top2 = jnp.concatenate(top2, axis=0)
        half = k // 2
        cand = [top1[0] + top2] + [top1[a] + top2[:half] for a in range(1, half)]
        cand.append(jnp.concatenate(top1[half:], axis=0) + top2[0:1])
        best, _ = _top_rows(jnp.concatenate(cand, axis=0), k + 1)
        z = jnp.ones_like(best[0])
        for r in best[1:k]:
            z = z + jnp.exp(r - best[0])
        cut = 0.5 * (best[k - 1] + best[k])
        n1 = jnp.zeros_like(s1)
        for b in range(k):
            n1 = n1 + jnp.where(s1 >= cut - top2[b:b + 1], 1.0, 0.0)
        n1_ref[hd] = n1
        c1_ref[hd] = jnp.where(s1 >= top1[-1], jnp.exp(s1 - top1[0]) / z, 0.0)
        r2_ref[hd] = rank2.astype(BF16)
        e2_ref[hd] = jnp.where(s2 >= top2[k - 1:], jnp.exp(s2 - top2[0:1]), 0.0).astype(BF16)


def _peer_route(h, wq, subkeys, tm=256):
    T, D = h.shape
    shape = (PEER_HEADS, PEER_N_KEYS, T)
    ospec = pl.BlockSpec((PEER_HEADS, PEER_N_KEYS, tm), lambda t: (0, 0, t))
    return pl.pallas_call(
        _route_kernel, grid=(T // tm,),
        in_specs=[pl.BlockSpec((tm, D), lambda t: (t, 0)),
                  pl.BlockSpec(wq.shape, lambda t: (0, 0)),
                  pl.BlockSpec(subkeys.shape, lambda t: (0, 0, 0, 0))],
        out_specs=[ospec] * 4,
        out_shape=[jax.ShapeDtypeStruct(shape, F32), jax.ShapeDtypeStruct(shape, F32),
                   jax.ShapeDtypeStruct(shape, BF16), jax.ShapeDtypeStruct(shape, BF16)],
        compiler_params=_cparams(1), name="peer_route")(h, wq, subkeys)


def _gelu(x):
    return 0.5 * x * (1.0 + lax.erf(x * (2.0 ** -0.5)))


PEER_TILE = 1024
MXU_DIM = 256
GATE_ROWS = 16


def _peer_kernel(h_ref, u_ref, vt_ref, n1_ref, c1_ref, r2_ref, e2_ref, o_ref,
                 xb_sc, acc_sc, part_sc, act_a, act_b, w_a, w_b, r2_sc, e2_sc, *, te, n_tiles):
    s = pl.program_id(1)
    tm, d = h_ref.shape

    @pl.when(s == 0)
    def _():
        xb_sc[...] = h_ref[...].astype(BF16)
        acc_sc[...] = jnp.zeros_like(acc_sc)
        r2_sc[...] = r2_ref[...]
        e2_sc[...] = e2_ref[...]

    per = te // PEER_N_KEYS
    n_k = d // MXU_DIM

    def step(act_new, act_old, w_new, w_old, stages):
        tile = s - 1

        def gate_piece(ii, lt):
            i = tile * per + ii
            lanes = slice(lt * LANES, (lt + 1) * LANES)
            nb = [jnp.broadcast_to(n1_ref[hd, pl.ds(i, 1), :][:, lanes].astype(BF16), (GATE_ROWS, LANES))
                  for hd in range(PEER_HEADS)]
            cb = [jnp.broadcast_to(c1_ref[hd, pl.ds(i, 1), :][:, lanes].astype(BF16), (GATE_ROWS, LANES))
                  for hd in range(PEER_HEADS)]
            zero = jnp.zeros((GATE_ROWS, LANES), BF16)
            for r0 in range(0, PEER_N_KEYS, GATE_ROWS):
                rows = slice(r0, r0 + GATE_ROWS)
                g = zero
                for hd in range(PEER_HEADS):
                    g = g + jnp.where(r2_sc[hd, rows, lanes] < nb[hd], e2_sc[hd, rows, lanes] * cb[hd], zero)
                arow = slice(ii * PEER_N_KEYS + r0, ii * PEER_N_KEYS + r0 + GATE_ROWS)
                w_new[arow, lanes] = g * act_old[arow, lanes]

        def stage1_piece(nc, kc):
            cols = slice(nc * MXU_DIM, (nc + 1) * MXU_DIM)
            kk = slice(kc * MXU_DIM, (kc + 1) * MXU_DIM)
            p = _dot_nt(u_ref[:, kk], xb_sc[cols, kk])
            if kc == 0:
                part_sc[:, cols] = p
            elif kc < n_k - 1:
                part_sc[:, cols] += p
            else:
                act_new[:, cols] = _gelu(part_sc[:, cols] + p).astype(BF16)

        def stage3_piece(nc, ec, mc):
            cols = slice(nc * MXU_DIM, (nc + 1) * MXU_DIM)
            ee = slice(ec * MXU_DIM, (ec + 1) * MXU_DIM)
            rr = slice(mc * (d // 2), (mc + 1) * (d // 2))
            acc_sc[rr, cols] += jnp.dot(vt_ref[rr, ee], w_old[ee, cols], preferred_element_type=F32)

        gates, mxu = [], []
        if 2 in stages:
            gates = [functools.partial(gate_piece, ii, lt) for ii in range(per) for lt in range(tm // LANES)]
        if 1 in stages:
            mxu += [functools.partial(stage1_piece, nc, kc) for nc in range(tm // MXU_DIM) for kc in range(n_k)]
        if 3 in stages:
            mxu += [functools.partial(stage3_piece, nc, ec, mc) for nc in range(tm // MXU_DIM)
                    for ec in range(te // MXU_DIM) for mc in range(2)]
        for b in range(max(len(gates), len(mxu))):
            if b < len(mxu):
                mxu[b]()
            if b < len(gates):
                gates[b]()

    act, w = (act_a, act_b), (w_a, w_b)
    steady = jnp.logical_and(s >= 2, s < n_tiles)
    variants = [(s == 0, 0, (1,)), (s == 1, 1, (1, 2)),
                (jnp.logical_and(steady, s % 2 == 0), 0, (1, 2, 3)),
                (jnp.logical_and(steady, s % 2 == 1), 1, (1, 2, 3)),
                (s == n_tiles, n_tiles % 2, (2, 3)), (s == n_tiles + 1, (n_tiles + 1) % 2, (3,))]
    for cond, par, stages in variants:
        @pl.when(cond)
        def _(par=par, stages=stages):
            step(act[par], act[1 - par], w[1 - par], w[par], stages)

    @pl.when(s == n_tiles + 1)
    def _():
        o_ref[...] = acc_sc[...].T


def _peer_experts(h, u_bf, vt_tiles, n1, c1, r2, e2, tm=512):
    T, D = h.shape
    n_tiles, _, te = vt_tiles.shape
    rspec = pl.BlockSpec((PEER_HEADS, PEER_N_KEYS, tm), lambda t, s: (0, 0, t))
    return pl.pallas_call(
        functools.partial(_peer_kernel, te=te, n_tiles=n_tiles), grid=(T // tm, n_tiles + 2),
        in_specs=[pl.BlockSpec((tm, D), lambda t, s: (t, 0)),
                  pl.BlockSpec((te, D), lambda t, s: (jnp.minimum(s, n_tiles - 1), 0)),
                  pl.BlockSpec((None, D, te), lambda t, s: (jnp.clip(s - 2, 0, n_tiles - 1), 0, 0)),
                  rspec, rspec, rspec, rspec],
        out_specs=pl.BlockSpec((tm, D), lambda t, s: (t, 0)),
        out_shape=jax.ShapeDtypeStruct((T, D), F32),
        scratch_shapes=[pltpu.VMEM((tm, D), BF16), pltpu.VMEM((D, tm), F32), pltpu.VMEM((te, tm), F32),
                        pltpu.VMEM((te, tm), BF16), pltpu.VMEM((te, tm), BF16),
                        pltpu.VMEM((te, tm), BF16), pltpu.VMEM((te, tm), BF16),
                        pltpu.VMEM((PEER_HEADS, PEER_N_KEYS, tm), BF16), pltpu.VMEM((PEER_HEADS, PEER_N_KEYS, tm), BF16)],
        compiler_params=_cparams(2), name="peer_experts")(h, u_bf, vt_tiles, n1, c1, r2, e2)


def _ple_kernel(h_ref, f_ref, p_ref, wg_ref, bg_ref, wp_ref, g_ref, beta_ref, o_ref, *, alpha):
    r = alpha * h_ref[...] + f_ref[...]
    z = jnp.dot(r.astype(BF16), wg_ref[...], preferred_element_type=F32) + bg_ref[...]
    gate = 1.0 / (1.0 + jnp.exp(-z))
    r = r + gate * jnp.dot(p_ref[...].astype(BF16), wp_ref[...], preferred_element_type=F32)
    o_ref[...] = _layer_norm(r, g_ref[...], beta_ref[...])


def _ple_ln2(h, f, p, layer, wg, bg, wp, g, beta, alpha, tm=512):
    T, D = h.shape
    pd = p.shape[-1]
    rows = pl.BlockSpec((tm, D), lambda t: (t, 0))
    vec = pl.BlockSpec((1, D), lambda t: (0, 0))
    return pl.pallas_call(
        functools.partial(_ple_kernel, alpha=alpha), grid=(T // tm,),
        in_specs=[rows, rows, pl.BlockSpec((None, tm, pd), lambda t: (layer, t, 0)),
                  pl.BlockSpec(wg.shape, lambda t: (0, 0)), vec,
                  pl.BlockSpec(wp.shape, lambda t: (0, 0)), vec, vec],
        out_specs=rows, out_shape=jax.ShapeDtypeStruct((T, D), F32),
        compiler_params=_cparams(1), name="ple_ln2")(
            h, f, p, wg, bg.reshape(1, D), wp, g.reshape(1, D), beta.reshape(1, D))


def _mixer_layer(h, B, S, w_in, w_out, lam_vecs, subln_g, lam_init, ln_g, ln_b, alpha):
    T, D = h.shape
    dw = D // 2
    n_diff = dw // (2 * DIFF_HEAD_DIM)
    sw = D - dw
    n_dsa = sw // DSA_HEAD_DIM
    n_iq = IDX_HEADS * IDX_DIM
    wb = w_in.astype(BF16)
    c_sv = 3 * dw + 2 * sw
    c_iq = c_sv + sw
    w_main = wb[:, :c_sv]
    w_vt = wb[:, c_sv:c_iq].T
    w_iq = wb[:, c_iq:c_iq + n_iq]
    w_ik = wb[:, c_iq + n_iq:c_iq + n_iq + IDX_DIM]
    w_iwt = jnp.pad(wb[:, c_iq + n_iq + IDX_DIM:].T, ((0, 8 - IDX_HEADS), (0, 0)))
    main, vt, iq, ik, iwt = _input_projection(h, w_main, w_vt, w_iq, w_ik, w_iwt)
    main = main.reshape(B, S, c_sv)
    a = _diff_attention(main, lam_vecs, subln_g, lam_init, n_diff, 0)
    b = _dsa_attention(main, vt, iq.reshape(B, S, -1), ik.reshape(B, S, -1), iwt, n_dsa,
                       3 * dw // sw, 3 * dw // sw + 1, 0)
    return _output_projection(a.reshape(T, -1), b.reshape(T, -1), h, w_out.astype(BF16), ln_g, ln_b, alpha)


def kernel(x, p, ln_in_g, ln_in_b, w_in, w_out, diff_lambda, diff_subln_g, ln1_g, ln1_b, peer_wq, peer_subkeys, peer_u, peer_v, ple_wg, ple_bg, ple_wp, ln2_g, ln2_b):
    B, S, D = x.shape
    depth = w_in.shape[0]
    T = B * S
    alpha = (2 * depth) ** 0.25
    h = _entry_layer_norm(x.reshape(T, D), ln_in_g, ln_in_b)
    p2 = p.reshape(depth, T, p.shape[-1])
    for i in range(depth):
        lam_init = 0.8 - 0.6 * math.exp(-0.3 * i)
        h = _mixer_layer(h, B, S, w_in[i], w_out[i], diff_lambda[i], diff_subln_g[i], lam_init,
                         ln1_g[i], ln1_b[i], alpha)
        n1, c1, r2, e2 = _peer_route(h, peer_wq[i].astype(BF16), peer_subkeys[i].astype(BF16))
        vt_tiles = peer_v[i].astype(BF16).reshape(-1, PEER_TILE, D).transpose(0, 2, 1)
        f = _peer_experts(h, peer_u[i].astype(BF16), vt_tiles, n1, c1, r2, e2)
        h = _ple_ln2(h, f, p2, i, ple_wg[i].astype(BF16), ple_bg[i], ple_wp[i].astype(BF16),
                     ln2_g[i], ln2_b[i], alpha)
    return h.reshape(B, S, D)
```

```python
import functools
import math
import struct

import jax
import jax.numpy as jnp
from jax import lax
from jax.experimental import pallas as pl
from jax.experimental.pallas import tpu as pltpu

F32 = jnp.float32
BF16 = jnp.bfloat16
I32 = jnp.int32

LN_EPS = 1e-5
NEG_INF = -1e30
CHUNK = 64
CHUNK_SHIFT = 6
LOG2E = math.log2(math.e)
LANES = 128
VMEM_LIMIT = 56 * 1024 * 1024

DIFF_HEAD_DIM = 64
DSA_HEAD_DIM = 64
IDX_HEADS = 4
IDX_DIM = 64
IDX_TOPK_MAX = 256
PEER_HEADS = 8
PEER_N_KEYS = 128
PEER_HALF_DIM = 128
PEER_TOPK = 16

_NEG_BITS = struct.unpack("<i", struct.pack("<f", NEG_INF))[0]
KEY_NEG = _NEG_BITS ^ 0x7FFFFFFF


def _cparams(n_axes):
    return pltpu.CompilerParams(
        dimension_semantics=("arbitrary",) * n_axes, vmem_limit_bytes=VMEM_LIMIT)


def _dot_nt(a, b):
    return lax.dot_general(a, b, (((1,), (1,)), ((), ())), preferred_element_type=F32)


def _layer_norm(x, g, b):
    mu = jnp.mean(x, axis=-1, keepdims=True)
    xc = x - mu
    var = jnp.mean(xc * xc, axis=-1, keepdims=True)
    return xc * lax.rsqrt(var + LN_EPS) * g + b


def _ln_kernel(x_ref, g_ref, b_ref, o_ref):
    o_ref[...] = _layer_norm(x_ref[...], g_ref[...], b_ref[...])


def _entry_layer_norm(x, g, b, tm=512):
    T, D = x.shape
    row = pl.BlockSpec((tm, D), lambda t: (t, 0))
    vec = pl.BlockSpec((1, D), lambda t: (0, 0))
    return pl.pallas_call(
        _ln_kernel, grid=(T // tm,), in_specs=[row, vec, vec], out_specs=row,
        out_shape=jax.ShapeDtypeStruct((T, D), F32), compiler_params=_cparams(1),
        name="entry_ln")(x, g.reshape(1, D), b.reshape(1, D))


def _inproj_kernel(h_ref, wm_ref, wvt_ref, wiq_ref, wik_ref, wiwt_ref, main_ref, vt_ref, iq_ref, ik_ref, iwt_ref, *, tn):
    hb = h_ref[...].astype(BF16)
    for j in range(0, wm_ref.shape[1], tn):
        main_ref[:, j:j + tn] = jnp.dot(hb, wm_ref[:, j:j + tn], preferred_element_type=F32).astype(BF16)
    vt_ref[0] = _dot_nt(wvt_ref[...], hb).astype(BF16)
    iq_ref[...] = jnp.dot(hb, wiq_ref[...], preferred_element_type=F32) * (IDX_DIM ** -0.5)
    ik_ref[...] = jnp.dot(hb, wik_ref[...], preferred_element_type=F32)
    iwt_ref[...] = _dot_nt(wiwt_ref[...], hb) * (IDX_HEADS ** -0.5)


def _input_projection(h, w_main, w_vt, w_iq, w_ik, w_iwt, tm=512, tn=512):
    T, D = h.shape
    nm = w_main.shape[1]
    nv = w_vt.shape[0]

    def full(w):
        return pl.BlockSpec(w.shape, lambda t: (0, 0))

    def rows(n):
        return pl.BlockSpec((tm, n), lambda t: (t, 0))

    return pl.pallas_call(
        functools.partial(_inproj_kernel, tn=tn), grid=(T // tm,),
        in_specs=[rows(D), full(w_main), full(w_vt), full(w_iq), full(w_ik), full(w_iwt)],
        out_specs=[rows(nm), pl.BlockSpec((1, nv, tm), lambda t: (t, 0, 0)),
                   rows(w_iq.shape[1]), rows(w_ik.shape[1]),
                   pl.BlockSpec((w_iwt.shape[0], tm), lambda t: (0, t))],
        out_shape=[jax.ShapeDtypeStruct((T, nm), BF16),
                   jax.ShapeDtypeStruct((T // tm, nv, tm), BF16),
                   jax.ShapeDtypeStruct((T, w_iq.shape[1]), F32),
                   jax.ShapeDtypeStruct((T, w_ik.shape[1]), F32),
                   jax.ShapeDtypeStruct((w_iwt.shape[0], T), F32)],
        compiler_params=_cparams(1), name="in_proj")(h, w_main, w_vt, w_iq, w_ik, w_iwt)


def _diff_kernel(lamv_ref, g_ref, q_ref, k_ref, v_ref, o_ref, qm_sc, acc_sc, *, tq, tk, lam_init, slopes):
    qi = pl.program_id(1)
    dv = 2 * DIFF_HEAD_DIM
    n_heads = len(slopes)
    lane = lax.broadcasted_iota(I32, (tq, dv), 1)
    for hd in range(n_heads):
        qs = q_ref[0, :, hd * dv:(hd + 1) * dv] * (DIFF_HEAD_DIM ** -0.5 * LOG2E)
        zero = jnp.zeros_like(qs)
        qm_sc[2 * hd] = jnp.where(lane < DIFF_HEAD_DIM, qs, zero)
        qm_sc[2 * hd + 1] = jnp.where(lane >= DIFF_HEAD_DIM, qs, zero)
    acc_sc[...] = jnp.zeros_like(acc_sc)
    t_idx = qi * tq + lax.broadcasted_iota(I32, (tq, tk), 0)

    def step(kb, carry, diagonal):
        off = pl.multiple_of(kb * tk, tk)
        s_idx = off + lax.broadcasted_iota(I32, (tq, tk), 1)
        dist = jnp.abs(t_idx - s_idx).astype(F32)
        if diagonal:
            visible = s_idx < (((t_idx >> CHUNK_SHIFT) + 1) << CHUNK_SHIFT)

        def qk(c):
            return _dot_nt(qm_sc[c], k_ref[0, pl.ds(off, tk), (c // 2) * dv:(c // 2 + 1) * dv])

        def pv(c, a, p):
            v = v_ref[0, pl.ds(off, tk), (c // 2) * dv:(c // 2 + 1) * dv]
            acc_sc[c] = a * acc_sc[c] + jnp.dot(p, v, preferred_element_type=F32)

        out = []
        s_next = qk(0)
        pending = None
        for c in range(2 * n_heads):
            s = s_next - (slopes[c // 2] * LOG2E) * dist
            if c + 1 < 2 * n_heads:
                s_next = qk(c + 1)
            if diagonal:
                s = jnp.where(visible, s, NEG_INF)
            m, l = carry[2 * c], carry[2 * c + 1]
            m_new = jnp.maximum(m, jnp.max(s, axis=-1, keepdims=True))
            p = jnp.exp2(s - m_new)
            a = jnp.exp2(m - m_new)
            out.extend((m_new, a * l + jnp.sum(p, axis=-1, keepdims=True)))
            if pending is not None:
                pv(*pending)
            pending = (c, a, p.astype(BF16))
        pv(*pending)
        return tuple(out)

    m0 = jnp.full((tq, 1), -jnp.inf, F32)
    l0 = jnp.zeros((tq, 1), F32)
    n_full = (qi * tq) // tk
    carry = lax.fori_loop(0, n_full, functools.partial(step, diagonal=False), (m0, l0) * (2 * n_heads))
    for d in range(max(tq // tk, 1)):
        carry = step(n_full + d, carry, True)

    lv = lamv_ref[...]
    lam = (jnp.exp(jnp.sum(lv[0:1] * lv[1:2], axis=-1, keepdims=True))
           - jnp.exp(jnp.sum(lv[2:3] * lv[3:4], axis=-1, keepdims=True)) + lam_init)
    for hd in range(n_heads):
        o = acc_sc[2 * hd] / carry[4 * hd + 1] - lam * (acc_sc[2 * hd + 1] / carry[4 * hd + 3])
        o = o * lax.rsqrt(jnp.mean(o * o, axis=-1, keepdims=True) + LN_EPS) * g_ref[...]
        o_ref[0, :, hd * dv:(hd + 1) * dv] = (o * (1.0 - lam_init)).astype(o_ref.dtype)


def _diff_attention(main, lam_vecs, subln_g, lam_init, n_heads, col0, tq=256, tk=512):
    B, S, _ = main.shape
    dv = 2 * DIFF_HEAD_DIM
    width = n_heads * dv
    slopes = tuple(2.0 ** (-8.0 * (i + 1) / n_heads) for i in range(n_heads))
    kv = lambda c: pl.BlockSpec((1, S, width), lambda b, q: (b, 0, c))
    qo = lambda c: pl.BlockSpec((1, tq, width), lambda b, q: (b, q, c))
    return pl.pallas_call(
        functools.partial(_diff_kernel, tq=tq, tk=tk, lam_init=lam_init, slopes=slopes),
        grid=(B, S // tq),
        in_specs=[pl.BlockSpec(lam_vecs.shape, lambda b, q: (0, 0)),
                  pl.BlockSpec((1, dv), lambda b, q: (0, 0)),
                  qo(col0), kv(col0 + 1), kv(col0 + 2)],
        out_specs=qo(0),
        out_shape=jax.ShapeDtypeStruct((B, S, width), BF16),
        scratch_shapes=[pltpu.VMEM((2 * n_heads, tq, dv), BF16), pltpu.VMEM((2 * n_heads, tq, dv), F32)],
        compiler_params=_cparams(2), name="diff_attn")(
            lam_vecs, subln_g.reshape(1, dv), main, main, main)


def _sort_key(x):
    bits = pltpu.bitcast(x + 0.0, I32)
    return jnp.where(bits < 0, bits ^ 0x7FFFFFFF, bits)


def _dsa_kernel(q_ref, k_ref, vt_ref, iq_ref, ik_ref, iwt_ref, o_ref, key_sc, mask_sc, qm_sc, acc_sc, *,
                tq, tk, seq, topk, slopes):
    qi = pl.program_id(1)
    q_pos0 = qi * tq
    nkb = (q_pos0 + tq + tk - 1) // tk
    n_tail = seq - nkb * tk
    t_row = q_pos0 + lax.broadcasted_iota(I32, (1, tq), 1)
    chunk_end = ((t_row >> CHUNK_SHIFT) + 1) << CHUNK_SHIFT
    s_col = lax.broadcasted_iota(I32, (tk, tq), 0)

    iq = iq_ref[0].astype(BF16)
    iq_heads = [iq[:, j * IDX_DIM:(j + 1) * IDX_DIM] for j in range(IDX_HEADS)]
    iw_rows = [iwt_ref[j:j + 1, :] for j in range(IDX_HEADS)]

    def score_block(kb, _):
        off = pl.multiple_of(kb * tk, tk)
        ik = ik_ref[0, pl.ds(off, tk), :].astype(BF16)
        acc = jnp.zeros((tk, tq), F32)
        for j in range(IDX_HEADS):
            acc = acc + iw_rows[j] * jnp.maximum(_dot_nt(ik, iq_heads[j]), 0.0)
        key_sc[kb] = _sort_key(jnp.where(off + s_col < chunk_end, acc, NEG_INF))
        return 0

    lax.fori_loop(0, nkb, score_block, 0)

    def count(pred):
        def body(kb, c):
            return c + jnp.sum(pred(key_sc[kb]).astype(I32), axis=0, keepdims=True)
        return lax.fori_loop(0, nkb, body, jnp.zeros((1, tq), I32))

    bits_per_check = 4

    def searching(st):
        i, _, done = st
        return jnp.logical_and(i < 32, jnp.min(done) == 0)

    def bit_steps(st):
        i, r, done = st
        for u in range(bits_per_check):
            cand = jnp.where(done > 0, r, r ^ (jnp.int32(1) << (31 - i - u)))
            cnt = count(lambda kk: kk >= cand) + jnp.where(cand <= KEY_NEG, n_tail, 0)
            r = jnp.where(cnt >= topk, cand, r)
            done = jnp.where(cnt == topk, 1, done)
        return i + bits_per_check, r, done

    _, tau, _ = lax.while_loop(
        searching, bit_steps, (jnp.int32(0), jnp.full((1, tq), -2 ** 31, I32), jnp.zeros((1, tq), I32)))
    cnt_gt = count(lambda kk: kk > tau) + jnp.where(tau < KEY_NEG, n_tail, 0)
    cnt_eq = count(lambda kk: kk == tau) + jnp.where(tau == KEY_NEG, n_tail, 0)
    need = topk - cnt_gt
    has_ties = jnp.max(cnt_eq - need) > 0

    @pl.when(jnp.logical_not(has_ties))
    def _():
        def body(kb, _):
            sel = jnp.logical_and(key_sc[kb] >= tau, kb * tk + s_col < chunk_end)
            mask_sc[kb] = jnp.where(sel, 0.0, NEG_INF)
            return 0
        lax.fori_loop(0, nkb, body, 0)

    @pl.when(has_ties)
    def _():
        r_i = lax.broadcasted_iota(I32, (tk, tk), 0)
        c_i = lax.broadcasted_iota(I32, (tk, tk), 1)
        earlier = jnp.where(c_i < r_i, 1.0, 0.0).astype(BF16)

        def body(kb, seen):
            kk = key_sc[kb]
            eq = kk == tau
            eqf = jnp.where(eq, 1.0, 0.0)
            rank = seen + jnp.dot(earlier, eqf.astype(BF16), preferred_element_type=F32)
            take = jnp.logical_and(eq, rank < need.astype(F32))
            sel = jnp.logical_and(jnp.logical_or(kk > tau, take), kb * tk + s_col < chunk_end)
            mask_sc[kb] = jnp.where(sel, 0.0, NEG_INF)
            return seen + jnp.sum(eqf, axis=0, keepdims=True)
        lax.fori_loop(0, nkb, body, jnp.zeros((1, tq), F32))

    hdim = DSA_HEAD_DIM
    pair = 2 * hdim
    n_heads = len(slopes)
    lane = lax.broadcasted_iota(I32, (tq, pair), 1)
    for p in range(n_heads // 2):
        qs = q_ref[0, :, p * pair:(p + 1) * pair] * (hdim ** -0.5 * LOG2E)
        zero = jnp.zeros_like(qs)
        qm_sc[2 * p] = jnp.where(lane < hdim, qs, zero)
        qm_sc[2 * p + 1] = jnp.where(lane >= hdim, qs, zero)
    acc_sc[...] = jnp.zeros_like(acc_sc)

    def attend(kb, carry):
        off = pl.multiple_of(kb * tk, tk)
        dist = jnp.abs(t_row - (off + s_col)).astype(F32)
        madd = mask_sc[kb]

        def qk(hd):
            k = k_ref[0, pl.ds(off, tk), (hd // 2) * pair:(hd // 2 + 1) * pair]
            return _dot_nt(k, qm_sc[hd])

        def pv(hd, a, pr):
            rows = slice(hd * hdim, (hd + 1) * hdim)
            acc_sc[rows, :] = a * acc_sc[rows, :] + jnp.dot(vt_ref[kb, rows, :], pr, preferred_element_type=F32)

        out = []
        s_next = qk(0)
        pending = None
        for hd in range(n_heads):
            s = s_next - (slopes[hd] * LOG2E) * dist + madd
            if hd + 1 < n_heads:
                s_next = qk(hd + 1)
            m, l = carry[2 * hd], carry[2 * hd + 1]
            m_new = jnp.maximum(m, jnp.max(s, axis=0, keepdims=True))
            pr = jnp.exp2(s - m_new)
            a = jnp.exp2(m - m_new)
            out.extend((m_new, a * l + jnp.sum(pr, axis=0, keepdims=True)))
            if pending is not None:
                pv(*pending)
            pending = (hd, a, pr.astype(BF16))
        pv(*pending)
        return tuple(out)

    m0 = jnp.full((1, tq), -jnp.inf, F32)
    l0 = jnp.zeros((1, tq), F32)
    carry = lax.fori_loop(0, nkb, attend, (m0, l0) * n_heads)
    for hd in range(n_heads):
        rows = slice(hd * hdim, (hd + 1) * hdim)
        acc_sc[rows, :] = acc_sc[rows, :] / carry[2 * hd + 1]
    o_ref[0] = acc_sc[...].T.astype(o_ref.dtype)


def _dsa_attention(main, vt, iq, ik, iwt, n_heads, col_q, col_k, vrow0, tq=512):
    B, S, _ = main.shape
    tk = vt.shape[-1]
    width = n_heads * DSA_HEAD_DIM
    nq = S // tq
    topk = min(IDX_TOPK_MAX, S // 4)
    slopes = tuple(2.0 ** (-8.0 * (i + 1) / n_heads) for i in range(n_heads))
    qrow = lambda n, c: pl.BlockSpec((1, tq, n), lambda b, q: (b, q, c))
    seqb = lambda n, c: pl.BlockSpec((1, S, n), lambda b, q: (b, 0, c))
    return pl.pallas_call(
        functools.partial(_dsa_kernel, tq=tq, tk=tk, seq=S, topk=topk, slopes=slopes),
        grid=(B, nq),
        in_specs=[qrow(width, col_q), seqb(width, col_k),
                  pl.BlockSpec((S // tk, width, tk), lambda b, q: (b, vrow0, 0)),
                  qrow(iq.shape[-1], 0), seqb(ik.shape[-1], 0),
                  pl.BlockSpec((iwt.shape[0], tq), lambda b, q: (0, b * nq + q))],
        out_specs=qrow(width, 0),
        out_shape=jax.ShapeDtypeStruct((B, S, width), BF16),
        scratch_shapes=[pltpu.VMEM((S // tk, tk, tq), I32), pltpu.VMEM((S // tk, tk, tq), F32),
                        pltpu.VMEM((n_heads, tq, 2 * DSA_HEAD_DIM), BF16),
                        pltpu.VMEM((width, tq), F32)],
        compiler_params=_cparams(2), name="dsa_attn")(main, main, vt, iq, ik, iwt)


def _outproj_kernel(a_ref, b_ref, h_ref, w_ref, g_ref, beta_ref, o_ref, *, alpha):
    wa = a_ref.shape[1]
    y = jnp.dot(a_ref[...], w_ref[:wa, :], preferred_element_type=F32)
    y = y + jnp.dot(b_ref[...], w_ref[wa:, :], preferred_element_type=F32)
    o_ref[...] = _layer_norm(alpha * h_ref[...] + y, g_ref[...], beta_ref[...])


def _output_projection(a, b, h, w_out, g, beta, alpha, tm=512):
    T, D = h.shape
    rows = lambda n: pl.BlockSpec((tm, n), lambda t: (t, 0))
    vec = pl.BlockSpec((1, D), lambda t: (0, 0))
    return pl.pallas_call(
        functools.partial(_outproj_kernel, alpha=alpha), grid=(T // tm,),
        in_specs=[rows(a.shape[1]), rows(b.shape[1]), rows(D),
                  pl.BlockSpec(w_out.shape, lambda t: (0, 0)), vec, vec],
        out_specs=rows(D), out_shape=jax.ShapeDtypeStruct((T, D), F32),
        compiler_params=_cparams(1), name="out_proj_ln1")(
            a, b, h, w_out, g.reshape(1, D), beta.reshape(1, D))


def _top_rows(x, n):
    rows = []
    rank = jnp.full(x.shape, float(n), F32)
    for b in range(n):
        mx = jnp.max(x, axis=0, keepdims=True)
        rows.append(mx)
        hit = x == mx
        rank = jnp.where(hit, float(b), rank)
        x = jnp.where(hit, -jnp.inf, x)
    return rows, rank


def _route_kernel(h_ref, wq_ref, sk_ref, n1_ref, c1_ref, r2_ref, e2_ref):
    hb = h_ref[...].astype(BF16)
    qd = 2 * PEER_HALF_DIM
    k = PEER_TOPK
    for hd in range(PEER_HEADS):
        q = jnp.dot(hb, wq_ref[:, hd * qd:(hd + 1) * qd], preferred_element_type=F32).astype(BF16)
        s1 = _dot_nt(sk_ref[hd, 0], q[:, :PEER_HALF_DIM])
        s2 = _dot_nt(sk_ref[hd, 1], q[:, PEER_HALF_DIM:])
        top1, _ = _top_rows(s1, k)
        top2, rank2 = _top_rows(s2, k)
        top2 = jnp.concatenate(top2, axis=0)
        half = k // 2
        cand = [top1[0] + top2] + [top1[a] + top2[:half] for a in range(1, half)]
        cand.append(jnp.concatenate(top1[half:], axis=0) + top2[0:1])
        best, _ = _top_rows(jnp.concatenate(cand, axis=0), k + 1)
        z = jnp.ones_like(best[0])
        for r in best[1:k]:
            z = z + jnp.exp(r - best[0])
        cut = 0.5 * (best[k - 1] + best[k])
        top1_rows = jnp.concatenate(top1, axis=0)
        n_sorted = jnp.zeros_like(top1_rows)
        for b in range(k):
            n_sorted = n_sorted + jnp.where(top1_rows >= cut - top2[b:b + 1], 1.0, 0.0)
        n1 = jnp.zeros_like(s1)
        for a in range(k):
            n1 = jnp.where(s1 == top1[a], n_sorted[a:a + 1], n1)
        n1_ref[hd] = n1
        c1_ref[hd] = jnp.where(s1 >= top1[-1], jnp.exp(s1 - top1[0]) / z, 0.0)
        r2_ref[hd] = rank2.astype(BF16)
        e2_ref[hd] = jnp.where(s2 >= top2[k - 1:], jnp.exp(s2 - top2[0:1]), 0.0).astype(BF16)


def _peer_route(h, wq, subkeys, tm=256):
    T, D = h.shape
    shape = (PEER_HEADS, PEER_N_KEYS, T)
    ospec = pl.BlockSpec((PEER_HEADS, PEER_N_KEYS, tm), lambda t: (0, 0, t))
    return pl.pallas_call(
        _route_kernel, grid=(T // tm,),
        in_specs=[pl.BlockSpec((tm, D), lambda t: (t, 0)),
                  pl.BlockSpec(wq.shape, lambda t: (0, 0)),
                  pl.BlockSpec(subkeys.shape, lambda t: (0, 0, 0, 0))],
        out_specs=[ospec] * 4,
        out_shape=[jax.ShapeDtypeStruct(shape, F32), jax.ShapeDtypeStruct(shape, F32),
                   jax.ShapeDtypeStruct(shape, BF16), jax.ShapeDtypeStruct(shape, BF16)],
        compiler_params=_cparams(1), name="peer_route")(h, wq, subkeys)


def _gelu(x):
    return 0.5 * x * (1.0 + lax.erf(x * (2.0 ** -0.5)))


PEER_TILE = 1024
MXU_DIM = 256
GATE_ROWS = 16


def _peer_kernel(h_ref, u_ref, vt_ref, n1_ref, c1_ref, r2_ref, e2_ref, o_ref,
                 xb_sc, acc_sc, part_sc, act_a, act_b, w_a, w_b, r2_sc, e2_sc, *, te, n_tiles):
    s = pl.program_id(1)
    tm, d = h_ref.shape

    @pl.when(s == 0)
    def _():
        xb_sc[...] = h_ref[...].astype(BF16)
        acc_sc[...] = jnp.zeros_like(acc_sc)
        r2_sc[...] = r2_ref[...]
        e2_sc[...] = e2_ref[...]

    per = te // PEER_N_KEYS
    n_k = d // MXU_DIM

    def step(act_new, act_old, w_new, w_old, stages):
        tile = s - 1

        def gate_piece(ii, lt):
            i = tile * per + ii
            lanes = slice(lt * LANES, (lt + 1) * LANES)
            nb = [jnp.broadcast_to(n1_ref[hd, pl.ds(i, 1), :][:, lanes].astype(BF16), (GATE_ROWS, LANES))
                  for hd in range(PEER_HEADS)]
            cb = [jnp.broadcast_to(c1_ref[hd, pl.ds(i, 1), :][:, lanes].astype(BF16), (GATE_ROWS, LANES))
                  for hd in range(PEER_HEADS)]
            zero = jnp.zeros((GATE_ROWS, LANES), BF16)
            for r0 in range(0, PEER_N_KEYS, GATE_ROWS):
                rows = slice(r0, r0 + GATE_ROWS)
                g = zero
                for hd in range(PEER_HEADS):
                    g = g + jnp.where(r2_sc[hd, rows, lanes] < nb[hd], e2_sc[hd, rows, lanes] * cb[hd], zero)
                arow = slice(ii * PEER_N_KEYS + r0, ii * PEER_N_KEYS + r0 + GATE_ROWS)
                w_new[arow, lanes] = g * act_old[arow, lanes]

        def stage1_piece(nc, kc):
            cols = slice(nc * MXU_DIM, (nc + 1) * MXU_DIM)
            kk = slice(kc * MXU_DIM, (kc + 1) * MXU_DIM)
            p = _dot_nt(u_ref[:, kk], xb_sc[cols, kk])
            if kc == 0:
                part_sc[:, cols] = p
            elif kc < n_k - 1:
                part_sc[:, cols] += p
            else:
                act_new[:, cols] = _gelu(part_sc[:, cols] + p).astype(BF16)

        def stage3_piece(nc, ec, mc):
            cols = slice(nc * MXU_DIM, (nc + 1) * MXU_DIM)
            ee = slice(ec * MXU_DIM, (ec + 1) * MXU_DIM)
            rr = slice(mc * (d // 2), (mc + 1) * (d // 2))
            acc_sc[rr, cols] += jnp.dot(vt_ref[rr, ee], w_old[ee, cols], preferred_element_type=F32)

        gates, mxu = [], []
        if 2 in stages:
            gates = [functools.partial(gate_piece, ii, lt) for ii in range(per) for lt in range(tm // LANES)]
        if 1 in stages:
            mxu += [functools.partial(stage1_piece, nc, kc) for nc in range(tm // MXU_DIM) for kc in range(n_k)]
        if 3 in stages:
            mxu += [functools.partial(stage3_piece, nc, ec, mc) for nc in range(tm // MXU_DIM)
                    for ec in range(te // MXU_DIM) for mc in range(2)]
        for b in range(max(len(gates), len(mxu))):
            if b < len(mxu):
                mxu[b]()
            if b < len(gates):
                gates[b]()

    act, w = (act_a, act_b), (w_a, w_b)
    steady = jnp.logical_and(s >= 2, s < n_tiles)
    variants = [(s == 0, 0, (1,)), (s == 1, 1, (1, 2)),
                (jnp.logical_and(steady, s % 2 == 0), 0, (1, 2, 3)),
                (jnp.logical_and(steady, s % 2 == 1), 1, (1, 2, 3)),
                (s == n_tiles, n_tiles % 2, (2, 3)), (s == n_tiles + 1, (n_tiles + 1) % 2, (3,))]
    for cond, par, stages in variants:
        @pl.when(cond)
        def _(par=par, stages=stages):
            step(act[par], act[1 - par], w[1 - par], w[par], stages)

    @pl.when(s == n_tiles + 1)
    def _():
        o_ref[...] = acc_sc[...].T


def _peer_experts(h, u_bf, vt_tiles, n1, c1, r2, e2, tm=512):
    T, D = h.shape
    n_tiles, _, te = vt_tiles.shape
    rspec = pl.BlockSpec((PEER_HEADS, PEER_N_KEYS, tm), lambda t, s: (0, 0, t))
    return pl.pallas_call(
        functools.partial(_peer_kernel, te=te, n_tiles=n_tiles), grid=(T // tm, n_tiles + 2),
        in_specs=[pl.BlockSpec((tm, D), lambda t, s: (t, 0)),
                  pl.BlockSpec((te, D), lambda t, s: (jnp.minimum(s, n_tiles - 1), 0)),
                  pl.BlockSpec((None, D, te), lambda t, s: (jnp.clip(s - 2, 0, n_tiles - 1), 0, 0)),
                  rspec, rspec, rspec, rspec],
        out_specs=pl.BlockSpec((tm, D), lambda t, s: (t, 0)),
        out_shape=jax.ShapeDtypeStruct((T, D), F32),
        scratch_shapes=[pltpu.VMEM((tm, D), BF16), pltpu.VMEM((D, tm), F32), pltpu.VMEM((te, tm), F32),
                        pltpu.VMEM((te, tm), BF16), pltpu.VMEM((te, tm), BF16),
                        pltpu.VMEM((te, tm), BF16), pltpu.VMEM((te, tm), BF16),
                        pltpu.VMEM((PEER_HEADS, PEER_N_KEYS, tm), BF16), pltpu.VMEM((PEER_HEADS, PEER_N_KEYS, tm), BF16)],
        compiler_params=_cparams(2), name="peer_experts")(h, u_bf, vt_tiles, n1, c1, r2, e2)


def _ple_kernel(h_ref, f_ref, p_ref, wg_ref, bg_ref, wp_ref, g_ref, beta_ref, o_ref, *, alpha):
    r = alpha * h_ref[...] + f_ref[...]
    z = jnp.dot(r.astype(BF16), wg_ref[...], preferred_element_type=F32) + bg_ref[...]
    gate = 1.0 / (1.0 + jnp.exp(-z))
    r = r + gate * jnp.dot(p_ref[...].astype(BF16), wp_ref[...], preferred_element_type=F32)
    o_ref[...] = _layer_norm(r, g_ref[...], beta_ref[...])


def _ple_ln2(h, f, p, layer, wg, bg, wp, g, beta, alpha, tm=512):
    T, D = h.shape
    pd = p.shape[-1]
    rows = pl.BlockSpec((tm, D), lambda t: (t, 0))
    vec = pl.BlockSpec((1, D), lambda t: (0, 0))
    return pl.pallas_call(
        functools.partial(_ple_kernel, alpha=alpha), grid=(T // tm,),
        in_specs=[rows, rows, pl.BlockSpec((None, tm, pd), lambda t: (layer, t, 0)),
                  pl.BlockSpec(wg.shape, lambda t: (0, 0)), vec,
                  pl.BlockSpec(wp.shape, lambda t: (0, 0)), vec, vec],
        out_specs=rows, out_shape=jax.ShapeDtypeStruct((T, D), F32),
        compiler_params=_cparams(1), name="ple_ln2")(
            h, f, p, wg, bg.reshape(1, D), wp, g.reshape(1, D), beta.reshape(1, D))


def _mixer_layer(h, B, S, w_in, w_out, lam_vecs, subln_g, lam_init, ln_g, ln_b, alpha):
    T, D = h.shape
    dw = D // 2
    n_diff = dw // (2 * DIFF_HEAD_DIM)
    sw = D - dw
    n_dsa = sw // DSA_HEAD_DIM
    n_iq = IDX_HEADS * IDX_DIM
    wb = w_in.astype(BF16)
    c_sv = 3 * dw + 2 * sw
    c_iq = c_sv + sw
    w_main = wb[:, :c_sv]
    w_vt = wb[:, c_sv:c_iq].T
    w_iq = wb[:, c_iq:c_iq + n_iq]
    w_ik = wb[:, c_iq + n_iq:c_iq + n_iq + IDX_DIM]
    w_iwt = jnp.pad(wb[:, c_iq + n_iq + IDX_DIM:].T, ((0, 8 - IDX_HEADS), (0, 0)))
    main, vt, iq, ik, iwt = _input_projection(h, w_main, w_vt, w_iq, w_ik, w_iwt)
    main = main.reshape(B, S, c_sv)
    a = _diff_attention(main, lam_vecs, subln_g, lam_init, n_diff, 0)
    b = _dsa_attention(main, vt, iq.reshape(B, S, -1), ik.reshape(B, S, -1), iwt, n_dsa,
                       3 * dw // sw, 3 * dw // sw + 1, 0)
    return _output_projection(a.reshape(T, -1), b.reshape(T, -1), h, w_out.astype(BF16), ln_g, ln_b, alpha)


def kernel(x, p, ln_in_g, ln_in_b, w_in, w_out, diff_lambda, diff_subln_g, ln1_g, ln1_b, peer_wq, peer_subkeys, peer_u, peer_v, ple_wg, ple_bg, ple_wp, ln2_g, ln2_b):
    B, S, D = x.shape
    depth = w_in.shape[0]
    T = B * S
    alpha = (2 * depth) ** 0.25
    h = _entry_layer_norm(x.reshape(T, D), ln_in_g, ln_in_b)
    p2 = p.reshape(depth, T, p.shape[-1])
    for i in range(depth):
        lam_init = 0.8 - 0.6 * math.exp(-0.3 * i)
        h = _mixer_layer(h, B, S, w_in[i], w_out[i], diff_lambda[i], diff_subln_g[i], lam_init,
                         ln1_g[i], ln1_b[i], alpha)
        n1, c1, r2, e2 = _peer_route(h, peer_wq[i].astype(BF16), peer_subkeys[i].astype(BF16))
        vt_tiles = peer_v[i].astype(BF16).reshape(-1, PEER_TILE, D).transpose(0, 2, 1)
        f = _peer_experts(h, peer_u[i].astype(BF16), vt_tiles, n1, c1, r2, e2)
        h = _ple_ln2(h, f, p2, i, ple_wg[i].astype(BF16), ple_bg[i], ple_wp[i].astype(BF16),
                     ln2_g[i], ln2_b[i], alpha)
    return h.reshape(B, S, D)
```
